```python
import jax, jax.numpy as jnp
from jax import lax
import numpy as np

D_MODEL = 1024
BATCH = 2
SEQ = 8192
DEPTH = 4
DEC_BATCH = 128
DEC_SEQ = 4
PAST_LEN = 2048
PAGE_SIZE = 128

H_A = 4
DK_A = 64
DV_A = 128
CHUNK_A = 128
ROPE_BASE_A = 10000.0
GN_EPS_A = 1e-5
DIL_GROUPS = ((128, 1), (512, 4), (2048, 16))
H_B = 4
DH_B = 64
BAND_B = 128
H_C = 8
N_C = 64
C_C = H_C * N_C
LORA_W = 64
LORA_A = 64
LORA_V = 32
LORA_G = 160
GN_EPS_C = 64e-5
D_FF = 2816
LN_EPS = 1e-5
ALPHA = (2 * DEPTH) ** 0.25
BETA = (8 * DEPTH) ** -0.25
A_QK = H_A * DK_A
A_V = H_A * DV_A
A_COLS = 2 * A_QK + 2 * A_V
B_W = H_B * DH_B
B_COLS = len(DIL_GROUPS) * 3 * B_W
C_OFF = A_COLS + B_COLS
C_SHIFT_W = 3 * C_C + LORA_W + LORA_A + LORA_G
G_OFF = C_OFF + C_SHIFT_W
IN_COLS = G_OFF + 3 * D_MODEL

kernel_name = 'hybrid_retention_dilattn_rwkv7_step'


def _layer_norm(x, g, b):
    xf = x.astype(jnp.float32)
    mu = jnp.mean(xf, -1, keepdims=True)
    var = jnp.mean(jnp.square(xf - mu), -1, keepdims=True)
    return ((xf - mu) * lax.rsqrt(var + LN_EPS)).astype(x.dtype) * g + b


def _head_norm(x, g, b, eps):
    xf = x.astype(jnp.float32)
    mu = jnp.mean(xf, -1, keepdims=True)
    var = jnp.mean(jnp.square(xf - mu), -1, keepdims=True)
    y = ((xf - mu) * lax.rsqrt(var + eps)).astype(x.dtype)
    return y.reshape(x.shape[:-2] + (-1,)) * g + b


def _swiglu(u, w_up, w_down):
    a, b = jnp.split(u @ w_up, 2, axis=-1)
    return (jax.nn.silu(a) * b) @ w_down


def _rotary(x, pos):
    half = x.shape[-1] // 2
    inv = 1.0 / (ROPE_BASE_A ** jnp.linspace(0.0, 1.0, half, dtype=jnp.float32))
    ang = pos.astype(jnp.float32)[:, None] * inv[None, :]
    cos = jnp.cos(ang)[None, :, None, :].astype(x.dtype)
    sin = jnp.sin(ang)[None, :, None, :].astype(x.dtype)
    x1, x2 = x[..., :half], x[..., half:]
    return jnp.concatenate([x1 * cos - x2 * sin, x1 * sin + x2 * cos], axis=-1)


def _retention(q, k, v, state, pos):
    n, t = q.shape[:2]
    c = CHUNK_A if t % CHUNK_A == 0 else t
    nc = t // c
    dt = q.dtype
    log_g = jnp.log(1.0 - 2.0 ** (-5.0 - jnp.arange(H_A, dtype=jnp.float32)))
    q = _rotary(q, pos).reshape(n, nc, c, H_A, DK_A)
    k = (_rotary(k, pos) * DK_A ** -0.5).reshape(n, nc, c, H_A, DK_A)
    v = v.reshape(n, nc, c, H_A, DV_A)
    idx = jnp.arange(c, dtype=jnp.float32)
    diff = idx[:, None] - idx[None, :]
    dmask = jnp.where(diff >= 0, jnp.exp(jnp.maximum(diff, 0.0)[None] * log_g[:, None, None]), 0.0).astype(dt)
    kdec = jnp.exp((c - 1 - idx)[None, :] * log_g[:, None]).astype(dt)
    qdec = jnp.exp((idx + 1)[None, :] * log_g[:, None]).astype(dt)
    gc = jnp.exp(c * log_g).astype(state.dtype)
    att = jnp.einsum('bnihd,bnjhd->bnhij', q, k) * dmask
    y = jnp.einsum('bnhij,bnjhe->bnihe', att, v)
    kv = jnp.einsum('bnjhd,hj,bnjhe->bnhde', k, kdec, v).astype(state.dtype)

    def step(r, kv_n):
        return r * gc[None, :, None, None] + kv_n, r

    r_fin, r_prev = lax.scan(step, state, jnp.moveaxis(kv, 1, 0))
    r_prev = jnp.moveaxis(r_prev, 0, 1).astype(dt)
    y = y + jnp.einsum('bnihd,hi,bnhde->bnihe', q, qdec, r_prev)
    return y.reshape(n, t, H_A, DV_A), r_fin


def _dilated_prompt(q, k, v, window, dil):
    n, s = q.shape[:2]
    ln = s // dil
    nb = -(-ln // BAND_B)
    lp = nb * BAND_B
    steps = window // dil

    def to_sub(z):
        z = z.reshape(n, ln, dil, H_B, DH_B).transpose(0, 2, 1, 3, 4)
        z = jnp.pad(z, ((0, 0), (0, 0), (0, lp - ln), (0, 0), (0, 0)))
        return z.reshape(n, dil, nb, BAND_B, H_B, DH_B)

    def with_prev(z):
        prev = jnp.pad(z, ((0, 0), (0, 0), (1, 0), (0, 0), (0, 0), (0, 0)))[:, :, :-1]
        return jnp.concatenate([prev, z], axis=3)

    qb = to_sub(q)
    kb = with_prev(to_sub(k))
    vb = with_prev(to_sub(v))
    sc = jnp.einsum('bdnqhe,bdnkhe->bdnhqk', qb, kb).astype(jnp.float32) * DH_B ** -0.5
    rel = (jnp.arange(BAND_B)[:, None] + BAND_B) - jnp.arange(2 * BAND_B)[None, :]
    key_idx = jnp.arange(nb)[:, None, None] * BAND_B + jnp.arange(2 * BAND_B)[None, None, :] - BAND_B
    valid = (rel >= 0) & (rel <= steps) & (key_idx >= 0)
    sc = jnp.where(valid[None, None, :, None], sc, -jnp.inf)
    lse = jax.nn.logsumexp(sc, axis=-1)
    p = jnp.exp(sc - lse[..., None]).astype(v.dtype)
    o = jnp.einsum('bdnhqk,bdnkhe->bdnqhe', p, vb)
    o = o.reshape(n, dil, lp, H_B, DH_B)[:, :, :ln].transpose(0, 2, 1, 3, 4).reshape(n, s, H_B, DH_B)
    lse = lse.transpose(0, 1, 2, 4, 3).reshape(n, dil, lp, H_B)[:, :, :ln].transpose(0, 2, 1, 3).reshape(n, s, H_B)
    return o, lse


def _dilated_sample(q, k, v, buf, window, dil):
    t = q.shape[1]
    wb = buf.shape[1]
    kall = jnp.concatenate([buf[:, :, 0], k], axis=1)
    vall = jnp.concatenate([buf[:, :, 1], v], axis=1)
    steps = window // dil
    src = wb + jnp.arange(t)[:, None] - dil * jnp.arange(steps + 1)[None, :]
    valid = src >= 0
    src = jnp.maximum(src, 0)
    kg = kall[:, src]
    vg = vall[:, src]
    sc = jnp.einsum('bthe,btjhe->bthj', q, kg).astype(jnp.float32) * DH_B ** -0.5
    sc = jnp.where(valid[None, :, None, :], sc, -jnp.inf)
    lse = jax.nn.logsumexp(sc, axis=-1)
    p = jnp.exp(sc - lse[..., None]).astype(v.dtype)
    o = jnp.einsum('bthj,btjhe->bthe', p, vg)
    new_buf = jnp.stack([kall[:, t:], vall[:, t:]], axis=2)
    return o, lse, new_buf


def _wkv_step(s, inp):
    r_t, w_t, k_t, v_t, kk_t, a_t = inp
    sa = jnp.einsum('bhvk,bhk->bhv', s, -kk_t)
    s = s * w_t[:, :, None, :] + sa[..., None] * (kk_t * a_t)[:, :, None, :] + v_t[..., None] * k_t[:, :, None, :]
    return s, jnp.einsum('bhvk,bhk->bhv', s, r_t)


def _rwkv7(p, shift_prev, state, v_first, mu, w0, w2, a0, a2, g2, k_k, k_a, r_k, lnx_g, lnx_b, vres):
    n, t, _ = p.shape
    prev = jnp.concatenate([shift_prev[:, None, :].astype(p.dtype), p[:, :-1]], axis=1)
    xs = p + (prev - p) * mu
    r, k, v, wl, al, gl = jnp.split(xs, [C_C, 2 * C_C, 3 * C_C, 3 * C_C + LORA_W, 3 * C_C + LORA_W + LORA_A], axis=-1)
    w = -jax.nn.softplus(-(w0 + jnp.tanh(wl) @ w2)) - 0.5
    decay = jnp.exp(-jnp.exp(w.astype(jnp.float32))).astype(p.dtype)
    a = jax.nn.sigmoid(a0 + al @ a2)
    g = jax.nn.sigmoid(gl) @ g2
    if vres is None:
        v_first = v
    else:
        v0, vr1, vr2 = vres
        v = v + (v_first - v) * jax.nn.sigmoid(v0 + (v @ vr1) @ vr2)
    r, k, v, decay, a = [z.reshape(n, t, H_C, N_C) for z in (r, k, v, decay, a)]
    kk = (k * k_k.reshape(H_C, N_C)).astype(jnp.float32)
    kk = (kk / jnp.maximum(jnp.sqrt(jnp.sum(kk * kk, -1, keepdims=True)), 1e-12)).astype(k.dtype)
    k = k * (1 + (a - 1) * k_a.reshape(H_C, N_C))
    seq = tuple(jnp.moveaxis(z, 1, 0).astype(state.dtype) for z in (r, decay, k, v, kk, a))
    s_fin, y = lax.scan(_wkv_step, state, seq)
    y = jnp.moveaxis(y, 0, 1).astype(p.dtype)
    bonus = (jnp.sum(r * k * r_k, -1, keepdims=True) * v).reshape(n, t, C_C)
    out = (_head_norm(y, lnx_g, lnx_b, GN_EPS_C) + bonus) * g
    return out, s_fin, p[:, -1], v_first


def _trunk(x, c, pos, past, W):
    n, t, d = x.shape
    dt = x.dtype
    ret_o, wkv_o, sh_o = [], [], []
    win_o = [[] for _ in DIL_GROUPS]
    v_first = None
    for l in range(DEPTH):
        m = (jax.nn.silu(c) @ W['w_ada'][l] + W['b_ada'][l]).reshape(n, 9, 1, d)
        u = x * (1 + m[:, 1]) + m[:, 0]
        h = _swiglu(u, W['ffn1_up'][l], W['ffn1_down'][l])
        x = _layer_norm(ALPHA * x + 0.5 * (1 + m[:, 2]) * h, W['ln_g'][l, 0], W['ln_b'][l, 0])
        u = x * (1 + m[:, 4]) + m[:, 3]
        proj = u @ W['w_in'][l]
        if past is None:
            r0 = jnp.zeros((n, H_A, DK_A, DV_A), dt)
            s0 = jnp.zeros((n, H_C, N_C, N_C), dt)
            sh0 = jnp.zeros((n, C_SHIFT_W), dt)
        else:
            r0, s0, sh0 = past[0][l], past[1][l], past[2][l]
        qa = proj[..., 0:A_QK].reshape(n, t, H_A, DK_A)
        ka = proj[..., A_QK:2 * A_QK].reshape(n, t, H_A, DK_A)
        va = proj[..., 2 * A_QK:2 * A_QK + A_V].reshape(n, t, H_A, DV_A)
        ga = proj[..., 2 * A_QK + A_V:A_COLS]
        ya, r_new = _retention(qa, ka, va, r0, pos)
        ya = _head_norm(ya, W['ret_gn_g'][l], W['ret_gn_b'][l], GN_EPS_A) * jax.nn.silu(ga)
        ob, lb = [], []
        for gi, (win, dil) in enumerate(DIL_GROUPS):
            base = A_COLS + gi * 3 * B_W
            qb = proj[..., base:base + B_W].reshape(n, t, H_B, DH_B)
            kb = proj[..., base + B_W:base + 2 * B_W].reshape(n, t, H_B, DH_B)
            vb = proj[..., base + 2 * B_W:base + 3 * B_W].reshape(n, t, H_B, DH_B)
            if past is None:
                o, lse = _dilated_prompt(qb, kb, vb, win, dil)
                wp = min(win, t)
                new_buf = jnp.stack([kb[:, t - wp:], vb[:, t - wp:]], axis=2)
            else:
                o, lse, new_buf = _dilated_sample(qb, kb, vb, past[3][gi][l], win, dil)
            ob.append(o)
            lb.append(lse)
            win_o[gi].append(new_buf)
        wts = jax.nn.softmax(jnp.stack(lb, 0), axis=0).astype(dt)
        yb = jnp.einsum('gnth,gnthe->nthe', wts, jnp.stack(ob, 0)).reshape(n, t, B_W)
        vres = None if l == 0 else (W['rw_v0'][l - 1], W['rw_vr1'][l - 1], W['rw_vr2'][l - 1])
        yc, s_new, sh_new, v_first = _rwkv7(
            proj[..., C_OFF:G_OFF], sh0, s0, v_first, W['rw_mu'][l], W['rw_w0'][l], W['rw_w2'][l],
            W['rw_a0'][l], W['rw_a2'][l], W['rw_g2'][l], W['rw_kk'][l], W['rw_ka'][l], W['rw_rk'][l],
            W['rw_lnx_g'][l], W['rw_lnx_b'][l], vres)
        gates = jax.nn.sigmoid(proj[..., G_OFF:] + W['b_gate'][l]).reshape(n, t, 3, d)
        mix = (gates[:, :, 0] * (ya @ W['w_proj_a'][l]) + gates[:, :, 1] * (yb @ W['w_proj_b'][l])
               + gates[:, :, 2] * (yc @ W['w_proj_c'][l]))
        x = _layer_norm(ALPHA * x + (1 + m[:, 5]) * (mix @ W['w_out'][l]), W['ln_g'][l, 1], W['ln_b'][l, 1])
        u = x * (1 + m[:, 7]) + m[:, 6]
        h = _swiglu(u, W['ffn2_up'][l], W['ffn2_down'][l])
        x = _layer_norm(ALPHA * x + 0.5 * (1 + m[:, 8]) * h, W['ln_g'][l, 2], W['ln_b'][l, 2])
        ret_o.append(r_new)
        wkv_o.append(s_new)
        sh_o.append(sh_new)
    return (x, jnp.stack(ret_o), jnp.stack(wkv_o), jnp.stack(sh_o),
            jnp.stack(win_o[0]), jnp.stack(win_o[1]), jnp.stack(win_o[2]))


def setup_inputs(seed: int = 0) -> dict:
    keys = iter(jax.random.split(jax.random.key(seed), 48))

    def nrm(shape, scale=1.0):
        return jax.random.normal(next(keys), shape, jnp.float32) * scale

    def unif(shape, lo, hi):
        return jax.random.uniform(next(keys), shape, jnp.float32, lo, hi)

    D, L = D_MODEL, DEPTH
    wins = [min(w, PAST_LEN) for w, _ in DIL_GROUPS]
    return {
        'x_prompt': nrm((BATCH, SEQ, D)),
        'x_sample': nrm((DEC_BATCH, DEC_SEQ, D)),
        'state_ret': nrm((L, DEC_BATCH, H_A, DK_A, DV_A), 0.1),
        'state_wkv': nrm((L, DEC_BATCH, H_C, N_C, N_C), 0.1),
        'state_shift': nrm((L, DEC_BATCH, C_SHIFT_W)),
        'cache_win1': nrm((L, DEC_BATCH, wins[0], 2, H_B, DH_B)),
        'cache_win2': nrm((L, DEC_BATCH, wins[1], 2, H_B, DH_B)),
        'cache_win3': nrm((L, DEC_BATCH, wins[2], 2, H_B, DH_B)),
        'c_prompt': nrm((BATCH, D)),
        'c_sample': nrm((DEC_BATCH, D)),
        'w_ada': nrm((L, D, 9 * D), 0.1 * D ** -0.5),
        'b_ada': nrm((L, 9 * D), 0.02),
        'ln_g': 1.0 + nrm((L, 3, D), 0.02),
        'ln_b': nrm((L, 3, D), 0.02),
        'ffn1_up': nrm((L, D, 2 * D_FF), D ** -0.5),
        'ffn1_down': nrm((L, D_FF, D), BETA * D_FF ** -0.5),
        'ffn2_up': nrm((L, D, 2 * D_FF), D ** -0.5),
        'ffn2_down': nrm((L, D_FF, D), BETA * D_FF ** -0.5),
        'w_in': nrm((L, D, IN_COLS), D ** -0.5),
        'b_gate': nrm((L, 3 * D), 0.01),
        'ret_gn_g': 1.0 + nrm((L, A_V), 0.02),
        'ret_gn_b': nrm((L, A_V), 0.02),
        'rw_mu': unif((L, C_SHIFT_W), 0.0, 1.0),
        'rw_w0': unif((L, C_C), -6.0, -1.0),
        'rw_w2': nrm((L, LORA_W, C_C), 0.1 * LORA_W ** -0.5),
        'rw_a0': nrm((L, C_C), 0.1),
        'rw_a2': nrm((L, LORA_A, C_C), 0.1 * LORA_A ** -0.5),
        'rw_g2': nrm((L, LORA_G, C_C), LORA_G ** -0.5),
        'rw_kk': 0.85 + nrm((L, C_C), 0.02),
        'rw_ka': 1.0 + nrm((L, C_C), 0.02),
        'rw_rk': nrm((L, H_C, N_C), 0.1),
        'rw_lnx_g': 1.0 + nrm((L, C_C), 0.02),
        'rw_lnx_b': nrm((L, C_C), 0.02),
        'rw_v0': nrm((L - 1, C_C), 0.1),
        'rw_vr1': nrm((L - 1, C_C, LORA_V), C_C ** -0.5),
        'rw_vr2': nrm((L - 1, LORA_V, C_C), 0.1 * LORA_V ** -0.5),
        'w_proj_a': nrm((L, A_V, D), A_V ** -0.5),
        'w_proj_b': nrm((L, B_W, D), B_W ** -0.5),
        'w_proj_c': nrm((L, C_C, D), C_C ** -0.5),
        'w_out': nrm((L, D, D), BETA * D ** -0.5),
    }


def reference(x_prompt, x_sample, state_ret, state_wkv, state_shift, cache_win1, cache_win2, cache_win3,
              c_prompt, c_sample, w_ada, b_ada, ln_g, ln_b, ffn1_up, ffn1_down, ffn2_up, ffn2_down,
              w_in, b_gate, ret_gn_g, ret_gn_b, rw_mu, rw_w0, rw_w2, rw_a0, rw_a2, rw_g2, rw_kk, rw_ka,
              rw_rk, rw_lnx_g, rw_lnx_b, rw_v0, rw_vr1, rw_vr2, w_proj_a, w_proj_b, w_proj_c, w_out):
    W = dict(w_ada=w_ada, b_ada=b_ada, ln_g=ln_g, ln_b=ln_b, ffn1_up=ffn1_up, ffn1_down=ffn1_down,
             ffn2_up=ffn2_up, ffn2_down=ffn2_down, w_in=w_in, b_gate=b_gate, ret_gn_g=ret_gn_g,
             ret_gn_b=ret_gn_b, rw_mu=rw_mu, rw_w0=rw_w0, rw_w2=rw_w2, rw_a0=rw_a0, rw_a2=rw_a2,
             rw_g2=rw_g2, rw_kk=rw_kk, rw_ka=rw_ka, rw_rk=rw_rk, rw_lnx_g=rw_lnx_g, rw_lnx_b=rw_lnx_b,
             rw_v0=rw_v0, rw_vr1=rw_vr1, rw_vr2=rw_vr2, w_proj_a=w_proj_a, w_proj_b=w_proj_b,
             w_proj_c=w_proj_c, w_out=w_out)
    pos_p = jnp.arange(x_prompt.shape[1])
    pos_s = PAST_LEN + jnp.arange(x_sample.shape[1])
    y_prompt, p_ret, p_wkv, p_shift, p_win1, p_win2, p_win3 = _trunk(x_prompt, c_prompt, pos_p, None, W)
    past = (state_ret, state_wkv, state_shift, (cache_win1, cache_win2, cache_win3))
    y_sample, s_ret, s_wkv, s_shift, s_win1, s_win2, s_win3 = _trunk(x_sample, c_sample, pos_s, past, W)
    return (y_prompt, y_sample, p_ret, p_wkv, p_shift, p_win1, p_win2, p_win3,
            s_ret, s_wkv, s_shift, s_win1, s_win2, s_win3)
```

```python
import functools
import math

import numpy as np
import jax
import jax.numpy as jnp
from jax import lax
from jax.experimental import pallas as pl
from jax.experimental.pallas import tpu as pltpu

F32 = jnp.float32
BF16 = jnp.bfloat16

H_A, DK_A, DV_A = 4, 64, 128
CHUNK_A = 128
ROPE_BASE_A = 10000.0
GN_EPS_A = 1e-5
DIL_GROUPS = ((128, 1), (512, 4), (2048, 16))
H_B, DH_B = 4, 64
BAND_B = 128
H_C, N_C = 8, 64
C_C = H_C * N_C
LORA_W, LORA_A, LORA_G = 64, 64, 160
GN_EPS_C = 64e-5
LN_EPS = 1e-5
DEPTH = 4
ALPHA = (2 * DEPTH) ** 0.25
A_QK = H_A * DK_A
A_V = H_A * DV_A
A_COLS = 2 * A_QK + 2 * A_V
B_W = H_B * DH_B
B_COLS = len(DIL_GROUPS) * 3 * B_W
C_OFF = A_COLS + B_COLS
LORA_COLS = LORA_W + LORA_A + LORA_G
C_SHIFT_W = 3 * C_C + LORA_COLS
G_OFF = C_OFF + C_SHIFT_W

LANES = 128
WKV_CHUNK = 64
ROW_TILE = 512
VMEM_LIMIT = 56 * 1024 * 1024


def _cparams(*sem):
    return pltpu.CompilerParams(dimension_semantics=sem, vmem_limit_bytes=VMEM_LIMIT)


def _dot(a, b):
    return jnp.dot(a, b, preferred_element_type=F32)


def _dot_nt(a, b):
    return lax.dot_general(a, b, (((1,), (1,)), ((), ())), preferred_element_type=F32)


def _dot_tn(a, b):
    return lax.dot_general(a, b, (((0,), (0,)), ((), ())), preferred_element_type=F32)


def _sigmoid(x):
    return 1.0 / (1.0 + jnp.exp(-x))


def _silu(x):
    return x * _sigmoid(x)


def _layer_norm(y, g, b):
    mu = jnp.mean(y, axis=-1, keepdims=True)
    d = y - mu
    var = jnp.mean(d * d, axis=-1, keepdims=True)
    return d * lax.rsqrt(var + LN_EPS) * g + b


def _resident(shape):
    nd = len(shape)
    return pl.BlockSpec(shape, lambda *_: (0,) * nd, pipeline_mode=pl.Buffered(1))


def _mod_spec(mod, tm, tiles_per_seq):
    if mod.ndim == 4:
        k, _, _, d = mod.shape
        return pl.BlockSpec((k, None, 1, d), lambda i: (0, i // tiles_per_seq, 0, 0))
    k, _, d = mod.shape
    return pl.BlockSpec((k, tm, d), lambda i: (0, i, 0))


def _ada_kernel(c_ref, w_ref, b_ref, o_ref):
    c = c_ref[...]
    o_ref[...] = _dot(_silu(c).astype(BF16), w_ref[...].astype(BF16)) + b_ref[...]


def _ada(c_all, w_ada, b_ada):
    nl, d, nd = w_ada.shape
    nc = c_all.shape[0]
    tn = 1024
    return pl.pallas_call(
        _ada_kernel,
        grid=(nl, nd // tn),
        in_specs=[
            pl.BlockSpec((nc, d), lambda l, j: (0, 0)),
            pl.BlockSpec((None, d, tn), lambda l, j: (l, 0, j)),
            pl.BlockSpec((None, 1, tn), lambda l, j: (l, 0, j)),
        ],
        out_specs=pl.BlockSpec((None, nc, tn), lambda l, j: (l, 0, j)),
        out_shape=jax.ShapeDtypeStruct((nl, nc, nd), F32),
        compiler_params=_cparams("parallel", "parallel"),
        name="ada",
    )(c_all, w_ada, b_ada.reshape(nl, 1, nd))


def _ffn_kernel(x_ref, mod_ref, wup_ref, wdn_ref, lng_ref, lnb_ref, o_ref, acc_ref, *, fc):
    x = x_ref[...]
    u = (x * (1.0 + mod_ref[1]) + mod_ref[0]).astype(BF16)
    dff = wdn_ref.shape[0]
    for c in range(dff // fc):
        a = _dot(u, wup_ref[:, c * fc:(c + 1) * fc])
        b = _dot(u, wup_ref[:, dff + c * fc:dff + (c + 1) * fc])
        g = (_silu(a) * b).astype(BF16)
        d = _dot(g, wdn_ref[c * fc:(c + 1) * fc, :])
        if c == 0:
            acc_ref[...] = d
        else:
            acc_ref[...] += d
    y = ALPHA * x + 0.5 * (1.0 + mod_ref[2]) * acc_ref[...]
    o_ref[...] = _layer_norm(y, lng_ref[...], lnb_ref[...])


def _ffn(x, mod, w_up, w_dn, ln_g, ln_b, tiles_per_seq):
    r, d = x.shape
    dff = w_dn.shape[0]
    tm = min(ROW_TILE, r)
    return pl.pallas_call(
        functools.partial(_ffn_kernel, fc=256),
        grid=(r // tm,),
        in_specs=[
            pl.BlockSpec((tm, d), lambda i: (i, 0)),
            _mod_spec(mod, tm, tiles_per_seq),
            _resident((d, 2 * dff)),
            _resident((dff, d)),
            _resident((1, d)),
            _resident((1, d)),
        ],
        out_specs=pl.BlockSpec((tm, d), lambda i: (i, 0)),
        out_shape=jax.ShapeDtypeStruct((r, d), F32),
        scratch_shapes=[pltpu.VMEM((tm, d), F32)],
        compiler_params=_cparams("parallel"),
        name="ffn",
    )(x, mod, w_up, w_dn, ln_g, ln_b)


def _swap_halves(x):
    lane = lax.broadcasted_iota(jnp.int32, x.shape, 1)
    fwd = pltpu.roll(x, LANES - 32, 1)
    bwd = pltpu.roll(x, 32, 1)
    return jnp.where((lane % 64) < 32, fwd, bwd)


def _proj_kernel(x_ref, mod_ref, w_ref, cos_ref, sin_ref,
                 qa_ref, ka_ref, va_ref, ga_ref, b0_ref, b1_ref, b2_ref, prkv_ref, plora_ref):
    x = x_ref[...]
    u = (x * (1.0 + mod_ref[1]) + mod_ref[0]).astype(BF16)
    cos = cos_ref[...]
    sin = sin_ref[...]

    def cols(c0, c1):
        return _dot(u, w_ref[:, c0:c1])

    for ref, base, scale in ((qa_ref, 0, 1.0), (ka_ref, A_QK, DK_A ** -0.5)):
        z = cols(base, base + A_QK)
        for t in range(A_QK // LANES):
            zt = z[:, t * LANES:(t + 1) * LANES]
            rot = zt * cos + _swap_halves(zt) * sin
            ref[:, t * LANES:(t + 1) * LANES] = (rot * scale).astype(ref.dtype)
    va_ref[...] = cols(2 * A_QK, 2 * A_QK + A_V).astype(va_ref.dtype)
    ga_ref[...] = cols(2 * A_QK + A_V, A_COLS).astype(ga_ref.dtype)
    for gi, ref in enumerate((b0_ref, b1_ref, b2_ref)):
        base = A_COLS + gi * 3 * B_W
        ref[...] = cols(base, base + 3 * B_W).astype(ref.dtype)
    prkv_ref[...] = cols(C_OFF, C_OFF + 3 * C_C)
    plora_ref[...] = cols(C_OFF + 3 * C_C, G_OFF)


def _proj(x, mod, w_mix, cos_t, sin_t, tiles_per_seq, b_dtype):
    r, d = x.shape
    tm = min(ROW_TILE, r)
    row = lambda w: pl.BlockSpec((tm, w), lambda i: (i, 0))
    tab = pl.BlockSpec((tm, LANES), lambda i: (i % tiles_per_seq, 0))
    outs = [(A_QK, BF16), (A_QK, BF16), (A_V, BF16), (A_V, BF16),
            (3 * B_W, b_dtype), (3 * B_W, b_dtype), (3 * B_W, b_dtype),
            (3 * C_C, F32), (LORA_COLS, F32)]
    return pl.pallas_call(
        _proj_kernel,
        grid=(r // tm,),
        in_specs=[row(d), _mod_spec(mod, tm, tiles_per_seq), _resident((d, G_OFF)), tab, tab],
        out_specs=[row(w) for w, _ in outs],
        out_shape=[jax.ShapeDtypeStruct((r, w), dt) for w, dt in outs],
        compiler_params=_cparams("parallel"),
        name="proj",
    )(x, mod, w_mix, cos_t, sin_t)


def _ret_prompt_kernel(q_ref, k_ref, v_ref, g_ref, dmask_ref, qdec_ref, kdec_ref, gc_ref, bd_ref,
                       gng_ref, gnb_ref, y_ref, rfin_ref, st_ref, *, nchunk):
    i = pl.program_id(1)

    @pl.when(i == 0)
    def _():
        st_ref[...] = jnp.zeros_like(st_ref)

    c = CHUNK_A
    lane_head = lax.broadcasted_iota(jnp.int32, (1, A_QK), 1) // DK_A
    head_sel = [(lane_head == h).astype(F32).astype(BF16) for h in range(H_A)]
    for ci in range(nchunk):
        rows = slice(ci * c, (ci + 1) * c)
        q = q_ref[rows, :]
        k = k_ref[rows, :]
        v = v_ref[rows, :]
        st = st_ref[...]
        ys = []
        for h in range(H_A):
            att = _dot_nt(q * head_sel[h], k) * dmask_ref[h]
            ys.append(_dot(att.astype(BF16), v[:, h * DV_A:(h + 1) * DV_A]))
        y = jnp.concatenate(ys, axis=1)
        y = y + _dot((q.astype(F32) * qdec_ref[...]).astype(BF16), st.astype(BF16))
        kv = _dot_tn((k.astype(F32) * kdec_ref[...]).astype(BF16), v)
        st_ref[...] = st * gc_ref[...] + kv * bd_ref[...]
        outs = []
        for h in range(H_A):
            yh = y[:, h * DV_A:(h + 1) * DV_A]
            mu = jnp.mean(yh, axis=-1, keepdims=True)
            dd = yh - mu
            var = jnp.mean(dd * dd, axis=-1, keepdims=True)
            outs.append(dd * lax.rsqrt(var + GN_EPS_A))
        yn = jnp.concatenate(outs, axis=1) * gng_ref[...] + gnb_ref[...]
        y_ref[rows, :] = (yn * _silu(g_ref[rows, :].astype(F32))).astype(y_ref.dtype)

    @pl.when(i == pl.num_programs(1) - 1)
    def _():
        rfin_ref[...] = st_ref[...]


def _ret_tables(c):
    log_g = jnp.log(1.0 - 2.0 ** (-5.0 - jnp.arange(H_A, dtype=F32)))
    idx = jnp.arange(c, dtype=F32)
    diff = idx[:, None] - idx[None, :]
    dmask = jnp.where(diff >= 0, jnp.exp(jnp.maximum(diff, 0.0)[None] * log_g[:, None, None]), 0.0)
    kdec = jnp.exp((c - 1 - idx)[None, :] * log_g[:, None])
    qdec = jnp.exp((idx + 1)[None, :] * log_g[:, None])
    gc = jnp.exp(c * log_g)
    return dmask, kdec, qdec, gc


def _ret_prompt(qa, ka, va, ga, gn_g, gn_b, n, s):
    tr = min(ROW_TILE, s)
    nt = s // tr
    dmask, kdec, qdec, gc = _ret_tables(CHUNK_A)
    qdec_t = jnp.repeat(qdec.T, DK_A, axis=1)
    kdec_t = jnp.repeat(kdec.T, DK_A, axis=1)
    gc_t = jnp.broadcast_to(jnp.repeat(gc, DK_A)[:, None], (A_QK, A_V))
    bd = (jnp.arange(A_QK)[:, None] // DK_A == jnp.arange(A_V)[None, :] // DV_A).astype(F32)
    row = lambda w: pl.BlockSpec((tr, w), lambda b, i: (b * nt + i, 0))
    y, rfin = pl.pallas_call(
        functools.partial(_ret_prompt_kernel, nchunk=tr // CHUNK_A),
        grid=(n, nt),
        in_specs=[row(A_QK), row(A_QK), row(A_V), row(A_V),
                  _resident((H_A, CHUNK_A, CHUNK_A)), _resident((CHUNK_A, A_QK)),
                  _resident((CHUNK_A, A_QK)), _resident((A_QK, A_V)), _resident((A_QK, A_V)),
                  _resident((1, A_V)), _resident((1, A_V))],
        out_specs=[row(A_V), pl.BlockSpec((None, A_QK, A_V), lambda b, i: (b, 0, 0))],
        out_shape=[jax.ShapeDtypeStruct((n * s, A_V), BF16),
                   jax.ShapeDtypeStruct((n, A_QK, A_V), F32)],
        scratch_shapes=[pltpu.VMEM((A_QK, A_V), F32)],
        compiler_params=_cparams("parallel", "arbitrary"),
        name="ret_prompt",
    )(qa, ka, va, ga, dmask, qdec_t, kdec_t, gc_t, bd, gn_g, gn_b)
    r4 = rfin.reshape(n, H_A, DK_A, H_A, DV_A)
    r_fin = jnp.stack([r4[:, h, :, h, :] for h in range(H_A)], axis=1)
    return y, r_fin


def _dil_prompt_kernel(q_ref, kc_ref, vc_ref, kp_ref, vp_ref, o_ref, lse_ref, *, nband):
    i = pl.program_id(2)
    bb = BAND_B
    qi = lax.broadcasted_iota(jnp.int32, (bb, 2 * bb), 0)
    ki = lax.broadcasted_iota(jnp.int32, (bb, 2 * bb), 1)
    band = (ki >= qi) & (ki <= qi + bb)
    lane_head = lax.broadcasted_iota(jnp.int32, (1, B_W), 1) // DH_B
    head_sel = [(lane_head == h).astype(F32).astype(BF16) for h in range(H_B)]
    first_key = jnp.where(i > 0, 0, bb)
    for j in range(nband):
        q = q_ref[j * bb:(j + 1) * bb, :]
        if j == 0:
            k = jnp.concatenate([kp_ref[...], kc_ref[0:bb, :]], axis=0)
            v = jnp.concatenate([vp_ref[...], vc_ref[0:bb, :]], axis=0)
            valid = band & (ki >= first_key)
        else:
            k = kc_ref[(j - 1) * bb:(j + 1) * bb, :]
            v = vc_ref[(j - 1) * bb:(j + 1) * bb, :]
            valid = band
        o = jnp.zeros((bb, B_W), F32)
        lse_full = jnp.zeros((bb, B_W), F32)
        for h in range(H_B):
            sc = _dot_nt(q * head_sel[h], k) * DH_B ** -0.5
            sc = jnp.where(valid, sc, -jnp.inf)
            m = jnp.max(sc, axis=-1, keepdims=True)
            p = jnp.exp(sc - m)
            l = jnp.sum(p, axis=-1, keepdims=True)
            o = o + _dot((p / l).astype(BF16), v * head_sel[h])
            lse_full = lse_full + jnp.where(lane_head == h, m + jnp.log(l), 0.0)
        o_ref[j * bb:(j + 1) * bb, :] = o.astype(o_ref.dtype)
        lse_ref[j * bb:(j + 1) * bb, :] = lse_full


def _dil_prompt(qkv, n, s, dil):
    ls = s // dil
    tq = min(ROW_TILE, ls)
    nband = tq // BAND_B
    view = qkv.reshape(n, ls, dil * 3 * B_W)
    cur = lambda c: pl.BlockSpec((None, tq, B_W), lambda b, r, i: (b, i, 3 * r + c))
    prev = lambda c: pl.BlockSpec((None, BAND_B, B_W),
                                  lambda b, r, i: (b, jnp.maximum(i * nband - 1, 0), 3 * r + c))
    out = pl.BlockSpec((None, tq, B_W), lambda b, r, i: (b, i, r))
    o, lse = pl.pallas_call(
        functools.partial(_dil_prompt_kernel, nband=nband),
        grid=(n, dil, ls // tq),
        in_specs=[cur(0), cur(1), cur(2), prev(1), prev(2)],
        out_specs=[out, out],
        out_shape=[jax.ShapeDtypeStruct((n, ls, dil * B_W), BF16),
                   jax.ShapeDtypeStruct((n, ls, dil * B_W), F32)],
        compiler_params=_cparams("parallel", "parallel", "arbitrary"),
        name="dil_prompt",
    )(view, view, view, view, view)
    return o.reshape(n * s, B_W), lse.reshape(n * s, B_W)


def _head_sum(x, bones):
    hi = x.astype(BF16)
    lo = (x - hi.astype(F32)).astype(BF16)
    return _dot(hi, bones) + _dot(lo, bones)


def _softplus(x):
    return jnp.maximum(x, 0.0) + jnp.log(1.0 + jnp.exp(-jnp.abs(x)))


def _rwkv_prep(prkv, plora, prev_rkv, prev_lora, prm, v_first, bones):
    xs = prkv + (prev_rkv - prkv) * prm["mu_rkv"][...]
    xl = plora + (prev_lora - plora) * prm["mu_lora"][...]
    r = xs[:, 0:C_C]
    k = xs[:, C_C:2 * C_C]
    v0 = xs[:, 2 * C_C:3 * C_C]
    wlog = -_softplus(-(prm["w0"][...] + _dot(jnp.tanh(xl).astype(BF16), prm["w2"][...]))) - 0.5
    lw = -jnp.exp(wlog)
    a = _sigmoid(prm["a0"][...] + _dot(xl.astype(BF16), prm["a2"][...]))
    g = _dot(_sigmoid(xl).astype(BF16), prm["g2"][...])
    if v_first is None:
        v = v0
    else:
        mixv = _dot(_dot(v0.astype(BF16), prm["vr1"][...]).astype(BF16), prm["vr2"][...])
        v = v0 + (v_first - v0) * _sigmoid(prm["v0"][...] + mixv)
    kk = k * prm["k_k"][...]
    kk = kk / jnp.maximum(jnp.sqrt(_head_sum(kk * kk, bones)), 1e-12)
    kmod = k * (1.0 + (a - 1.0) * prm["k_a"][...])
    bonus = _head_sum(r * kmod * prm["r_k"][...], bones) * v
    return r, lw, kmod, v, -kk, kk * a, g, bonus, v0


def _rwkv_post(y, bonus, g, prm, bones):
    mu = _head_sum(y, bones) * (1.0 / N_C)
    d = y - mu
    var = _head_sum(d * d, bones) * (1.0 / N_C)
    yn = d * lax.rsqrt(var + GN_EPS_C) * prm["lnx_g"][...] + prm["lnx_b"][...]
    return (yn + bonus) * g


_RWKV_PARAM_NAMES = ("mu_rkv", "mu_lora", "w0", "w2", "a0", "a2", "g2", "k_k", "k_a", "r_k",
                     "lnx_g", "lnx_b", "v0", "vr1", "vr2")


def _shift_rows(x, first_row):
    rolled = pltpu.roll(x, 1, 0)
    row = lax.broadcasted_iota(jnp.int32, x.shape, 0)
    return jnp.where(row == 0, first_row, rolled)


def _wkv_chunk(row0, s_ref, y_ref, r_ref, lw_ref, kmod_ref, v_ref, nkk_ref, beta_ref, ltri):
    c = WKV_CHUNK
    hw = N_C
    rows = pl.ds(row0, c)
    lw_all = lw_ref[rows, :]
    h1 = lw_all.astype(BF16)
    r1 = lw_all - h1.astype(F32)
    h2 = r1.astype(BF16)
    h3 = (r1 - h2.astype(F32)).astype(BF16)
    cs_all = _dot(ltri, h1) + _dot(ltri, h2) + _dot(ltri, h3)

    ti = lax.broadcasted_iota(jnp.int32, (c, 2 * hw), 0)
    si = lax.broadcasted_iota(jnp.int32, (c, 2 * hw), 1) % hw
    strict = si < ti
    incl = si <= ti
    ipair = (si == ti).astype(F32)
    bdm = (lax.broadcasted_iota(jnp.int32, (2 * c, 2 * hw), 0) // c
           == lax.broadcasted_iota(jnp.int32, (2 * c, 2 * hw), 1) // hw)
    bdm4 = jnp.concatenate([bdm, bdm], axis=0)

    def bd(x):
        return jnp.where(bdm, jnp.concatenate([x, x], axis=0), 0.0).astype(BF16)

    for p in range(C_C // LANES):
        sl = slice(p * LANES, (p + 1) * LANES)
        cs = cs_all[:, sl]
        lw = lw_all[:, sl]
        cl = cs[c - 1:c, :]
        inv_p = jnp.exp(-cs)
        dec_end = jnp.exp(cl - cs)
        beta = beta_ref[rows, sl]
        kmod = kmod_ref[rows, sl]
        v = v_ref[rows, sl]
        a_t = nkk_ref[rows, sl] * jnp.exp(cs - lw)
        r_t = r_ref[rows, sl] * jnp.exp(cs)
        b_t = beta * inv_p
        k_t = kmod * inv_p
        x1 = jnp.concatenate([a_t, r_t], axis=0).astype(BF16)
        bk = jnp.where(bdm4, jnp.concatenate([b_t, b_t, k_t, k_t], axis=0), 0.0).astype(BF16)
        gm = _dot_nt(x1, bk)
        a_ab = jnp.where(strict, gm[0:c, 0:2 * hw], 0.0)
        a_ak = jnp.where(strict, gm[0:c, 2 * hw:4 * hw], 0.0)
        a_rb = jnp.where(incl, gm[c:2 * c, 0:2 * hw], 0.0)
        a_rk = jnp.where(incl, gm[c:2 * c, 2 * hw:4 * hw], 0.0)
        npow = _dot(a_ab.astype(BF16), bd(a_ab))
        tm = ipair + a_ab
        nsq = int(math.log2(c))
        for j in range(1, nsq):
            stacked = jnp.concatenate([npow, tm], axis=0).astype(BF16)
            res = _dot(stacked, bd(npow))
            tm = tm + res[c:2 * c, :]
            npow = res[0:c, :]
        w1 = _dot(a_ak.astype(BF16), bd(v))
        res2 = _dot(tm.astype(BF16), jnp.concatenate([bd(a_t), bd(w1)], axis=1))
        a_hat = res2[:, 0:2 * hw]
        u_hat = res2[:, 2 * hw:4 * hw]
        s0 = s_ref[p]
        res3 = _dot_nt(jnp.concatenate([a_hat, r_t], axis=0).astype(BF16), s0.astype(BF16))
        u = res3[0:c, :] + u_hat
        y = res3[c:2 * c, :] + _dot(jnp.concatenate([a_rb, a_rk], axis=1).astype(BF16),
                                    jnp.concatenate([bd(u), bd(v)], axis=0))
        y_ref[rows, sl] = y
        upd = _dot_tn(jnp.concatenate([u, v], axis=0).astype(BF16),
                      jnp.concatenate([beta * dec_end, kmod * dec_end], axis=0).astype(BF16))
        s_ref[p] = s0 * jnp.exp(cl) + jnp.where(bdm, upd, 0.0)


def _rwkv_prompt_kernel(*refs, has_vres, nchunk):
    it = iter(refs)
    prkv_ref, plora_ref = next(it), next(it)
    vfirst_ref = next(it) if has_vres else None
    bones_ref, ltri_ref = next(it), next(it)
    prm = {name: next(it) for name in _RWKV_PARAM_NAMES}
    yc_ref = next(it)
    vout_ref = None if has_vres else next(it)
    s_ref = next(it)
    carry_rkv, carry_lora = next(it), next(it)
    r_s, lw_s, kmod_s, v_s, nkk_s, beta_s, y_s = (next(it) for _ in range(7))

    i = pl.program_id(1)

    @pl.when(i == 0)
    def _():
        carry_rkv[...] = jnp.zeros_like(carry_rkv)
        carry_lora[...] = jnp.zeros_like(carry_lora)
        s_ref[...] = jnp.zeros_like(s_ref)

    prkv = prkv_ref[...]
    plora = plora_ref[...]
    tr = prkv.shape[0]
    bones = bones_ref[...]
    r, lw, kmod, v, nkk, beta, g, bonus, v0 = _rwkv_prep(
        prkv, plora, _shift_rows(prkv, carry_rkv[...]), _shift_rows(plora, carry_lora[...]),
        prm, vfirst_ref[...] if has_vres else None, bones)
    carry_rkv[...] = prkv[tr - 1:tr, :]
    carry_lora[...] = plora[tr - 1:tr, :]
    r_s[...] = r
    lw_s[...] = lw
    kmod_s[...] = kmod
    v_s[...] = v
    nkk_s[...] = nkk
    beta_s[...] = beta
    if not has_vres:
        vout_ref[...] = v0
    ltri = ltri_ref[...]

    def body(ci, carry):
        row0 = pl.multiple_of(ci * WKV_CHUNK, WKV_CHUNK)
        _wkv_chunk(row0, s_ref, y_s, r_s, lw_s, kmod_s, v_s, nkk_s, beta_s, ltri)
        return carry

    lax.fori_loop(0, nchunk, body, 0)
    yc_ref[...] = _rwkv_post(y_s[...], bonus, g, prm, bones).astype(yc_ref.dtype)


def _rwkv_consts():
    bones = (jnp.arange(C_C)[:, None] // N_C == jnp.arange(C_C)[None, :] // N_C).astype(BF16)
    ltri = (jnp.arange(WKV_CHUNK)[:, None] >= jnp.arange(WKV_CHUNK)[None, :]).astype(BF16)
    return bones, ltri


def _rwkv_prompt(prkv, plora, v_first, prm, n, s):
    tr = min(ROW_TILE, s)
    nt = s // tr
    has_vres = v_first is not None
    bones, ltri = _rwkv_consts()
    row = lambda w: pl.BlockSpec((tr, w), lambda b, i: (b * nt + i, 0))
    ins = [prkv, plora] + ([v_first] if has_vres else []) + [bones, ltri] + [prm[k] for k in _RWKV_PARAM_NAMES]
    in_specs = ([row(3 * C_C), row(LORA_COLS)] + ([row(C_C)] if has_vres else [])
                + [_resident(a.shape) for a in ins[(3 if has_vres else 2):]])
    npair = C_C // LANES
    st_spec = pl.BlockSpec((None, npair, LANES, LANES), lambda b, i: (b, 0, 0, 0))
    st_shape = jax.ShapeDtypeStruct((n, npair, LANES, LANES), F32)
    vout_spec = [] if has_vres else [row(C_C)]
    vout_shape = [] if has_vres else [jax.ShapeDtypeStruct((n * s, C_C), F32)]
    res = pl.pallas_call(
        functools.partial(_rwkv_prompt_kernel, has_vres=has_vres, nchunk=tr // WKV_CHUNK),
        grid=(n, nt),
        in_specs=in_specs,
        out_specs=[row(C_C)] + vout_spec + [st_spec],
        out_shape=[jax.ShapeDtypeStruct((n * s, C_C), BF16)] + vout_shape + [st_shape],
        scratch_shapes=[pltpu.VMEM((1, 3 * C_C), F32), pltpu.VMEM((1, LORA_COLS), F32)]
                       + [pltpu.VMEM((tr, C_C), F32) for _ in range(7)],
        compiler_params=_cparams("parallel", "arbitrary"),
        name="rwkv_prompt",
    )(*ins)
    yc, s_bd = res[0], res[-1]
    vout = None if has_vres else res[1]
    s6 = s_bd.reshape(n, npair, 2, N_C, 2, N_C)
    s_fin = jnp.stack([s6[:, :, 0, :, 0, :], s6[:, :, 1, :, 1, :]], axis=2).reshape(n, H_C, N_C, N_C)
    return yc, vout, s_fin


def _mix_kernel(x_ref, mod_ref, ya_ref, o0_ref, o1_ref, o2_ref, l0_ref, l1_ref, l2_ref, yc_ref,
                wg_ref, bg_ref, wa_ref, wb_ref, wc_ref, wo_ref, lng_ref, lnb_ref, out_ref):
    x = x_ref[...]
    d = x.shape[1]
    u = (x * (1.0 + mod_ref[1]) + mod_ref[0]).astype(BF16)
    l0, l1, l2 = l0_ref[...], l1_ref[...], l2_ref[...]
    m = jnp.maximum(jnp.maximum(l0, l1), l2)
    e0, e1, e2 = jnp.exp(l0 - m), jnp.exp(l1 - m), jnp.exp(l2 - m)
    den = e0 + e1 + e2
    yb = ((e0 / den) * o0_ref[...].astype(F32) + (e1 / den) * o1_ref[...].astype(F32)
          + (e2 / den) * o2_ref[...].astype(F32))
    branches = (_dot(ya_ref[...].astype(BF16), wa_ref[...]),
                _dot(yb.astype(BF16), wb_ref[...]),
                _dot(yc_ref[...].astype(BF16), wc_ref[...]))
    mix = None
    for j, br in enumerate(branches):
        gate = _sigmoid(_dot(u, wg_ref[:, j * d:(j + 1) * d]) + bg_ref[:, j * d:(j + 1) * d])
        mix = gate * br if mix is None else mix + gate * br
    h = _dot(mix.astype(BF16), wo_ref[...])
    y = ALPHA * x + (1.0 + mod_ref[2]) * h
    out_ref[...] = _layer_norm(y, lng_ref[...], lnb_ref[...])


def _mix(x, mod, ya, ob, lb, yc, wg, bg, wa, wb, wc, wo, ln_g, ln_b, tiles_per_seq):
    r, d = x.shape
    tm = min(ROW_TILE, r)
    row = lambda w: pl.BlockSpec((tm, w), lambda i: (i, 0))
    ins = [x, mod, ya, ob[0], ob[1], ob[2], lb[0], lb[1], lb[2], yc, wg, bg, wa, wb, wc, wo, ln_g, ln_b]
    in_specs = ([row(d), _mod_spec(mod, tm, tiles_per_seq), row(A_V)] + [row(B_W)] * 6 + [row(C_C)]
                + [_resident(a.shape) for a in ins[10:]])
    return pl.pallas_call(
        _mix_kernel,
        grid=(r // tm,),
        in_specs=in_specs,
        out_specs=row(d),
        out_shape=jax.ShapeDtypeStruct((r, d), F32),
        compiler_params=_cparams("parallel"),
        name="mix",
    )(*ins)


def _ret_sample_kernel(tab_ref, q_ref, k_ref, v_ref, st_ref, y_ref, so_ref, *, t):
    h = pl.program_id(0)
    dm = lambda i, j: tab_ref[h, i * t + j]
    kdec = lambda j: tab_ref[h, t * t + j]
    qdec = lambda i: tab_ref[h, t * t + t + i]
    gc = tab_ref[h, t * t + 2 * t]
    att = [[jnp.sum(q_ref[i] * k_ref[j], axis=0, keepdims=True) * dm(i, j) for j in range(i + 1)]
           for i in range(t)]
    for i in range(t):
        def cross(d, acc, i=i):
            return acc + q_ref[i, pl.ds(d, 1), :] * st_ref[d]
        y = lax.fori_loop(0, DK_A, cross, jnp.zeros((DV_A, q_ref.shape[-1]), F32), unroll=4) * qdec(i)
        for j in range(i + 1):
            y = y + att[i][j] * v_ref[j]
        y_ref[i] = y

    def upd(d, carry):
        acc = st_ref[d] * gc
        for j in range(t):
            acc = acc + (k_ref[j, pl.ds(d, 1), :] * kdec(j)) * v_ref[j]
        so_ref[d] = acc
        return carry

    lax.fori_loop(0, DK_A, upd, 0, unroll=2)


def _ret_sample(q, k, v, state, t):
    nb = q.shape[-1]
    dmask, kdec, qdec, gc = _ret_tables(t)
    tab = jnp.concatenate([dmask.reshape(H_A, t * t), kdec, qdec, gc[:, None]], axis=1)
    tab = jnp.pad(tab, ((0, 0), (0, 128 - tab.shape[1])))
    per_head = lambda rows: pl.BlockSpec((t, None, rows, nb), lambda h: (0, h, 0, 0))
    st_spec = pl.BlockSpec((None, DK_A, DV_A, nb), lambda h: (h, 0, 0, 0))
    return pl.pallas_call(
        functools.partial(_ret_sample_kernel, t=t),
        grid=(H_A,),
        in_specs=[pl.BlockSpec(memory_space=pltpu.SMEM), per_head(DK_A), per_head(DK_A), per_head(DV_A), st_spec],
        out_specs=[per_head(DV_A), st_spec],
        out_shape=[jax.ShapeDtypeStruct((t, H_A, DV_A, nb), F32),
                   jax.ShapeDtypeStruct(state.shape, F32)],
        compiler_params=_cparams("parallel"),
        name="ret_sample",
    )(tab, q, k, v, state)


def _ret_norm_kernel(y_ref, g_ref, gng_ref, gnb_ref, o_ref):
    y = y_ref[...]
    outs = []
    for h in range(H_A):
        yh = y[:, h * DV_A:(h + 1) * DV_A]
        mu = jnp.mean(yh, axis=-1, keepdims=True)
        dd = yh - mu
        var = jnp.mean(dd * dd, axis=-1, keepdims=True)
        outs.append(dd * lax.rsqrt(var + GN_EPS_A))
    yn = jnp.concatenate(outs, axis=1) * gng_ref[...] + gnb_ref[...]
    o_ref[...] = (yn * _silu(g_ref[...].astype(F32))).astype(o_ref.dtype)


def _ret_norm(y, g, gn_g, gn_b):
    r = y.shape[0]
    full = lambda a: pl.BlockSpec(a.shape, lambda: (0,) * a.ndim)
    return pl.pallas_call(
        _ret_norm_kernel,
        in_specs=[full(y), full(g), full(gn_g), full(gn_b)],
        out_specs=pl.BlockSpec((r, A_V), lambda: (0, 0)),
        out_shape=jax.ShapeDtypeStruct((r, A_V), BF16),
        name="ret_norm",
    )(y, g, gn_g, gn_b)


def _dil_sample_kernel(q0_ref, q1_ref, q2_ref, w1_ref, w2_ref, w3_ref,
                       o0_ref, o1_ref, o2_ref, l0_ref, l1_ref, l2_ref, *, sb, t):
    nrow = 8
    head_row = lax.broadcasted_iota(jnp.int32, (nrow, B_W), 0)
    lane_head = lax.broadcasted_iota(jnp.int32, (nrow, B_W), 1) // DH_B
    hsel = head_row == lane_head
    key_row = lax.broadcasted_iota(jnp.int32, (nrow, BAND_B), 1)
    new_col = lax.broadcasted_iota(jnp.int32, (nrow, nrow), 1)
    kvw = 2 * B_W
    groups = ((q0_ref, w1_ref, o0_ref, l0_ref, 1), (q1_ref, w2_ref, o1_ref, l1_ref, 4),
              (q2_ref, w3_ref, o2_ref, l2_ref, 16))
    for b in range(sb):
        win0 = (b * t) // nrow * nrow
        off = b * t - win0
        for q_ref, w_ref, o_ref, l_ref, dil in groups:
            new = q_ref[win0:win0 + nrow, :]
            k_new = new[:, B_W:2 * B_W].astype(BF16)
            v_new = new[:, 2 * B_W:3 * B_W].astype(BF16)
            for ti in range(t):
                q = new[off + ti:off + ti + 1, 0:B_W]
                qm = jnp.where(hsel, jnp.broadcast_to(q, (nrow, B_W)), 0.0).astype(BF16)
                lane0 = 0 if dil == 1 else ti * kvw
                kc = w_ref[b, :, lane0:lane0 + B_W].astype(BF16)
                vc = w_ref[b, :, lane0 + B_W:lane0 + kvw].astype(BF16)
                sc = _dot_nt(qm, kc) * DH_B ** -0.5
                sn = _dot_nt(qm, k_new) * DH_B ** -0.5
                tok = new_col - off
                if dil == 1:
                    sc = jnp.where(key_row >= ti, sc, -jnp.inf)
                    ok_new = (tok >= 0) & (tok <= ti)
                else:
                    ok_new = tok == ti
                sn = jnp.where(ok_new, sn, -jnp.inf)
                m = jnp.maximum(jnp.max(sc, axis=-1, keepdims=True), jnp.max(sn, axis=-1, keepdims=True))
                p = jnp.exp(sc - m)
                pn = jnp.exp(sn - m)
                l = jnp.sum(p, axis=-1, keepdims=True) + jnp.sum(pn, axis=-1, keepdims=True)
                o = (_dot(p.astype(BF16), vc) + _dot(pn.astype(BF16), v_new)) / l
                row = b * t + ti
                o_ref[row:row + 1, :] = jnp.sum(jnp.where(hsel, o, 0.0), axis=0, keepdims=True).astype(o_ref.dtype)
                l_ref[row:row + 1, :] = jnp.sum(jnp.where(hsel, m + jnp.log(l), 0.0), axis=0, keepdims=True)


def _dil_sample(qkv, wins, layer, nb, t):
    sb = 4
    kvw = 2 * B_W
    views = [w.reshape(w.shape[0], nb, BAND_B, (w.shape[2] // BAND_B) * kvw) for w in wins]
    qspec = pl.BlockSpec((sb * t, 3 * B_W), lambda i: (i, 0))
    wspec = lambda v: pl.BlockSpec((None, sb, BAND_B, min(v.shape[3], t * kvw)), lambda i: (layer, i, 0, 0))
    ospec = pl.BlockSpec((sb * t, B_W), lambda i: (i, 0))
    outs = pl.pallas_call(
        functools.partial(_dil_sample_kernel, sb=sb, t=t),
        grid=(nb // sb,),
        in_specs=[qspec, qspec, qspec] + [wspec(v) for v in views],
        out_specs=[ospec] * 6,
        out_shape=[jax.ShapeDtypeStruct((nb * t, B_W), F32)] * 6,
        compiler_params=_cparams("parallel"),
        name="dil_sample",
    )(*qkv, *views)
    return outs[:3], outs[3:]


def _rwkv_prep_kernel(*refs, has_vres):
    it = iter(refs)
    prkv_ref, plora_ref, prev_rkv_ref, prev_lora_ref = (next(it) for _ in range(4))
    vfirst_ref = next(it) if has_vres else None
    bones_ref = next(it)
    prm = {name: next(it) for name in _RWKV_PARAM_NAMES}
    outs = [next(it) for _ in range(9)]
    res = _rwkv_prep(prkv_ref[...], plora_ref[...], prev_rkv_ref[...], prev_lora_ref[...], prm,
                     vfirst_ref[...] if has_vres else None, bones_ref[...])
    r, lw, kmod, v, nkk, beta, g, bonus, v0 = res
    for ref, val in zip(outs, (r, jnp.exp(lw), kmod, v, nkk, beta, g, bonus, v0)):
        ref[...] = val


def _rwkv_prep_call(prkv, plora, prev_rkv, prev_lora, v_first, prm):
    has_vres = v_first is not None
    bones, _ = _rwkv_consts()
    ins = [prkv, plora, prev_rkv, prev_lora] + ([v_first] if has_vres else []) + [bones] + [prm[k] for k in _RWKV_PARAM_NAMES]
    full = lambda a: pl.BlockSpec(a.shape, lambda: (0,) * a.ndim)
    r = prkv.shape[0]
    return pl.pallas_call(
        functools.partial(_rwkv_prep_kernel, has_vres=has_vres),
        in_specs=[full(a) for a in ins],
        out_specs=[pl.BlockSpec((r, C_C), lambda: (0, 0))] * 9,
        out_shape=[jax.ShapeDtypeStruct((r, C_C), F32)] * 9,
        compiler_params=pltpu.CompilerParams(vmem_limit_bytes=VMEM_LIMIT),
        name="rwkv_prep",
    )(*ins)


def _wkv_sample_kernel(r_ref, w_ref, k_ref, v_ref, nkk_ref, beta_ref, st_ref, y_ref, so_ref, *, t):
    nb = st_ref.shape[-1]
    for ti in range(t):
        src = st_ref if ti == 0 else so_ref

        def p1(kc, sa, src=src, ti=ti):
            return sa + src[kc] * nkk_ref[ti, pl.ds(kc, 1), :]

        sa = lax.fori_loop(0, N_C, p1, jnp.zeros((N_C, nb), F32), unroll=4)
        vt = v_ref[ti]

        def p2(kc, y, src=src, ti=ti, sa=sa, vt=vt):
            row = pl.ds(kc, 1)
            s_new = (src[kc] * w_ref[ti, row, :] + sa * beta_ref[ti, row, :] + vt * k_ref[ti, row, :])
            so_ref[kc] = s_new
            return y + s_new * r_ref[ti, row, :]

        y_ref[ti] = lax.fori_loop(0, N_C, p2, jnp.zeros((N_C, nb), F32), unroll=4)


def _wkv_sample(r, w, k, v, nkk, beta, state, t):
    nb = r.shape[-1]
    per_head = pl.BlockSpec((t, None, N_C, nb), lambda h: (0, h, 0, 0))
    st_spec = pl.BlockSpec((None, N_C, N_C, nb), lambda h: (h, 0, 0, 0))
    return pl.pallas_call(
        functools.partial(_wkv_sample_kernel, t=t),
        grid=(H_C,),
        in_specs=[per_head] * 6 + [st_spec],
        out_specs=[per_head, st_spec],
        out_shape=[jax.ShapeDtypeStruct((t, H_C, N_C, nb), F32), jax.ShapeDtypeStruct(state.shape, F32)],
        compiler_params=_cparams("parallel"),
        name="wkv_sample",
    )(r, w, k, v, nkk, beta, state)


def _rwkv_post_kernel(y_ref, bonus_ref, g_ref, bones_ref, lnxg_ref, lnxb_ref, o_ref):
    prm = {"lnx_g": lnxg_ref, "lnx_b": lnxb_ref}
    o_ref[...] = _rwkv_post(y_ref[...], bonus_ref[...], g_ref[...], prm, bones_ref[...]).astype(o_ref.dtype)


def _rwkv_post_call(y, bonus, g, prm):
    bones, _ = _rwkv_consts()
    ins = [y, bonus, g, bones, prm["lnx_g"], prm["lnx_b"]]
    full = lambda a: pl.BlockSpec(a.shape, lambda: (0,) * a.ndim)
    r = y.shape[0]
    return pl.pallas_call(
        _rwkv_post_kernel,
        in_specs=[full(a) for a in ins],
        out_specs=pl.BlockSpec((r, C_C), lambda: (0, 0)),
        out_shape=jax.ShapeDtypeStruct((r, C_C), BF16),
        name="rwkv_post",
    )(*ins)


def _rotary_tables(pos):
    half = DK_A // 2
    inv = 1.0 / (ROPE_BASE_A ** jnp.linspace(0.0, 1.0, half, dtype=F32))
    ang = pos.astype(F32)[:, None] * inv[None, :]
    cos, sin = jnp.cos(ang), jnp.sin(ang)
    return jnp.tile(cos, (1, 4)), jnp.tile(jnp.concatenate([-sin, sin], axis=1), (1, 2))


def _rwkv_params(l, w):
    row = lambda a: a.reshape(1, -1)
    pad_rows = lambda a, r0: jnp.zeros((LORA_COLS, a.shape[1]), BF16).at[r0:r0 + a.shape[0]].set(a.astype(BF16))
    prm = {
        "mu_rkv": row(w["rw_mu"][l, :3 * C_C]), "mu_lora": row(w["rw_mu"][l, 3 * C_C:]),
        "w0": row(w["rw_w0"][l]), "w2": pad_rows(w["rw_w2"][l], 0),
        "a0": row(w["rw_a0"][l]), "a2": pad_rows(w["rw_a2"][l], LORA_W),
        "g2": pad_rows(w["rw_g2"][l], LORA_W + LORA_A),
        "k_k": row(w["rw_kk"][l]), "k_a": row(w["rw_ka"][l]), "r_k": row(w["rw_rk"][l]),
        "lnx_g": row(w["rw_lnx_g"][l]), "lnx_b": row(w["rw_lnx_b"][l]),
    }
    lv = max(l - 1, 0)
    prm["v0"] = row(w["rw_v0"][lv])
    prm["vr1"] = w["rw_vr1"][lv].astype(BF16)
    prm["vr2"] = w["rw_vr2"][lv].astype(BF16)
    return prm


def _layer_weights(l, w):
    d = w["w_in"].shape[1]
    return dict(
        ffn1_up=w["ffn1_up"][l].astype(BF16), ffn1_dn=w["ffn1_down"][l].astype(BF16),
        ffn2_up=w["ffn2_up"][l].astype(BF16), ffn2_dn=w["ffn2_down"][l].astype(BF16),
        w_mix=w["w_in"][l, :, :G_OFF].astype(BF16), w_gate=w["w_in"][l, :, G_OFF:].astype(BF16),
        b_gate=w["b_gate"][l].reshape(1, -1),
        wa=w["w_proj_a"][l].astype(BF16), wb=w["w_proj_b"][l].astype(BF16),
        wc=w["w_proj_c"][l].astype(BF16), wo=w["w_out"][l].astype(BF16),
        ln_g=[w["ln_g"][l, j].reshape(1, d) for j in range(3)],
        ln_b=[w["ln_b"][l, j].reshape(1, d) for j in range(3)],
        gn_g=w["ret_gn_g"][l].reshape(1, -1), gn_b=w["ret_gn_b"][l].reshape(1, -1),
        rwkv=_rwkv_params(l, w),
    )


def _prompt_layer(x, m, lw, tabs, v_first, n, s):
    tps = s // min(ROW_TILE, s)
    x = _ffn(x, m[0:3], lw["ffn1_up"], lw["ffn1_dn"], lw["ln_g"][0], lw["ln_b"][0], tps)
    qa, ka, va, ga, b0, b1, b2, prkv, plora = _proj(x, m[3:6], lw["w_mix"], tabs[0], tabs[1], tps, BF16)
    ya, r_fin = _ret_prompt(qa, ka, va, ga, lw["gn_g"], lw["gn_b"], n, s)
    ob, lb, wins = [], [], []
    for qkv, (win, dil) in zip((b0, b1, b2), DIL_GROUPS):
        o, lse = _dil_prompt(qkv, n, s, dil)
        ob.append(o)
        lb.append(lse)
        wp = min(win, s)
        kv = qkv.reshape(n, s, 3, H_B, DH_B)[:, s - wp:, 1:3].astype(F32)
        wins.append(kv)
    yc, v_out, s_fin = _rwkv_prompt(prkv, plora, v_first, lw["rwkv"], n, s)
    if v_first is None:
        v_first = v_out
    sh_new = jnp.concatenate([prkv.reshape(n, s, -1)[:, -1], plora.reshape(n, s, -1)[:, -1]], axis=-1)
    x = _mix(x, m[3:6], ya, ob, lb, yc, lw["w_gate"], lw["b_gate"], lw["wa"], lw["wb"], lw["wc"], lw["wo"],
             lw["ln_g"][1], lw["ln_b"][1], tps)
    x = _ffn(x, m[6:9], lw["ffn2_up"], lw["ffn2_dn"], lw["ln_g"][2], lw["ln_b"][2], tps)
    return x, v_first, (r_fin, s_fin, sh_new, wins)


def _to_lanes(a, nb, t, heads):
    return a.reshape(nb, t, heads, -1).transpose(1, 2, 3, 0)


def _from_lanes(a):
    t, heads, w, nb = a.shape
    return a.transpose(3, 0, 1, 2).reshape(nb * t, heads * w)


def _sample_layer(x, m, lw, tabs, v_first, l, past, nb, t):
    state_ret, state_wkv, state_shift, caches = past
    x = _ffn(x, m[0:3], lw["ffn1_up"], lw["ffn1_dn"], lw["ln_g"][0], lw["ln_b"][0], 1)
    qa, ka, va, ga, b0, b1, b2, prkv, plora = _proj(x, m[3:6], lw["w_mix"], tabs[0], tabs[1], 1, F32)
    y_l, r_new_l = _ret_sample(_to_lanes(qa.astype(F32), nb, t, H_A), _to_lanes(ka.astype(F32), nb, t, H_A),
                               _to_lanes(va.astype(F32), nb, t, H_A),
                               state_ret[l].transpose(1, 2, 3, 0), t)
    r_new = r_new_l.transpose(3, 0, 1, 2)
    ya = _ret_norm(_from_lanes(y_l), ga, lw["gn_g"], lw["gn_b"])
    ob, lb = _dil_sample((b0, b1, b2), caches, l, nb, t)
    wins = []
    for qkv, cache in zip((b0, b1, b2), caches):
        kv_new = qkv.reshape(nb, t, 3, H_B, DH_B)[:, :, 1:3]
        wins.append(jnp.concatenate([cache[l][:, t:], kv_new], axis=1))
    prev_rkv = jnp.concatenate([state_shift[l][:, None, :3 * C_C], prkv.reshape(nb, t, -1)[:, :-1]], axis=1)
    prev_lora = jnp.concatenate([state_shift[l][:, None, 3 * C_C:], plora.reshape(nb, t, -1)[:, :-1]], axis=1)
    r, dec, kmod, v, nkk, beta, g, bonus, v0 = _rwkv_prep_call(
        prkv, plora, prev_rkv.reshape(nb * t, -1), prev_lora.reshape(nb * t, -1), v_first, lw["rwkv"])
    if v_first is None:
        v_first = v0
    y_l, s_new_l = _wkv_sample(*[_to_lanes(a, nb, t, H_C) for a in (r, dec, kmod, v, nkk, beta)],
                               state_wkv[l].transpose(1, 3, 2, 0), t)
    s_new = s_new_l.transpose(3, 0, 2, 1)
    yc = _rwkv_post_call(_from_lanes(y_l), bonus, g, lw["rwkv"])
    sh_new = jnp.concatenate([prkv.reshape(nb, t, -1)[:, -1], plora.reshape(nb, t, -1)[:, -1]], axis=-1)
    x = _mix(x, m[3:6], ya, ob, lb, yc, lw["w_gate"], lw["b_gate"], lw["wa"], lw["wb"], lw["wc"], lw["wo"],
             lw["ln_g"][1], lw["ln_b"][1], 1)
    x = _ffn(x, m[6:9], lw["ffn2_up"], lw["ffn2_dn"], lw["ln_g"][2], lw["ln_b"][2], 1)
    return x, v_first, (r_new, s_new, sh_new, wins)


def kernel(x_prompt, x_sample, state_ret, state_wkv, state_shift, cache_win1, cache_win2, cache_win3, c_prompt, c_sample, w_ada, b_ada, ln_g, ln_b, ffn1_up, ffn1_down, ffn2_up, ffn2_down, w_in, b_gate, ret_gn_g, ret_gn_b, rw_mu, rw_w0, rw_w2, rw_a0, rw_a2, rw_g2, rw_kk, rw_ka, rw_rk, rw_lnx_g, rw_lnx_b, rw_v0, rw_vr1, rw_vr2, w_proj_a, w_proj_b, w_proj_c, w_out):
    w = dict(w_ada=w_ada, b_ada=b_ada, ln_g=ln_g, ln_b=ln_b, ffn1_up=ffn1_up, ffn1_down=ffn1_down,
             ffn2_up=ffn2_up, ffn2_down=ffn2_down, w_in=w_in, b_gate=b_gate, ret_gn_g=ret_gn_g,
             ret_gn_b=ret_gn_b, rw_mu=rw_mu, rw_w0=rw_w0, rw_w2=rw_w2, rw_a0=rw_a0, rw_a2=rw_a2,
             rw_g2=rw_g2, rw_kk=rw_kk, rw_ka=rw_ka, rw_rk=rw_rk, rw_lnx_g=rw_lnx_g, rw_lnx_b=rw_lnx_b,
             rw_v0=rw_v0, rw_vr1=rw_vr1, rw_vr2=rw_vr2, w_proj_a=w_proj_a, w_proj_b=w_proj_b,
             w_proj_c=w_proj_c, w_out=w_out)
    n, s, d = x_prompt.shape
    nb, t, _ = x_sample.shape
    depth = w_ada.shape[0]
    past_len = cache_win3.shape[2]

    m_all = _ada(jnp.concatenate([c_prompt, c_sample], axis=0), w_ada, b_ada).reshape(depth, n + nb, 9, d)
    tabs_p = _rotary_tables(jnp.arange(s))
    tabs_s = tuple(jnp.tile(tb, (nb, 1)) for tb in _rotary_tables(past_len + jnp.arange(t)))
    past = (state_ret, state_wkv, state_shift, (cache_win1, cache_win2, cache_win3))

    xp = x_prompt.reshape(n * s, d)
    xs = x_sample.reshape(nb * t, d)
    vf_p = vf_s = None
    outs_p, outs_s = [], []
    for l in range(depth):
        lw = _layer_weights(l, w)
        m_p = m_all[l, :n].transpose(1, 0, 2)[:, :, None, :]
        m_s = jnp.repeat(m_all[l, n:], t, axis=0).transpose(1, 0, 2)
        xp, vf_p, st_p = _prompt_layer(xp, m_p, lw, tabs_p, vf_p, n, s)
        xs, vf_s, st_s = _sample_layer(xs, m_s, lw, tabs_s, vf_s, l, past, nb, t)
        outs_p.append(st_p)
        outs_s.append(st_s)

    def collect(outs):
        ret = jnp.stack([o[0] for o in outs])
        wkv = jnp.stack([o[1] for o in outs])
        sh = jnp.stack([o[2] for o in outs])
        wins = [jnp.stack([o[3][gi] for o in outs]) for gi in range(len(DIL_GROUPS))]
        return ret, wkv, sh, wins

    p_ret, p_wkv, p_sh, p_wins = collect(outs_p)
    s_ret, s_wkv, s_sh, s_wins = collect(outs_s)
    return (xp.reshape(n, s, d), xs.reshape(nb, t, d), p_ret, p_wkv, p_sh, *p_wins,
            s_ret, s_wkv, s_sh, *s_wins)
```

```python
import functools
import math

import numpy as np
import jax
import jax.numpy as jnp
from jax import lax
from jax.experimental import pallas as pl
from jax.experimental.pallas import tpu as pltpu

F32 = jnp.float32
BF16 = jnp.bfloat16

H_A, DK_A, DV_A = 4, 64, 128
CHUNK_A = 128
ROPE_BASE_A = 10000.0
GN_EPS_A = 1e-5
DIL_GROUPS = ((128, 1), (512, 4), (2048, 16))
H_B, DH_B = 4, 64
BAND_B = 128
H_C, N_C = 8, 64
C_C = H_C * N_C
LORA_W, LORA_A, LORA_G = 64, 64, 160
GN_EPS_C = 64e-5
LN_EPS = 1e-5
DEPTH = 4
ALPHA = (2 * DEPTH) ** 0.25
A_QK = H_A * DK_A
A_V = H_A * DV_A
A_COLS = 2 * A_QK + 2 * A_V
B_W = H_B * DH_B
B_COLS = len(DIL_GROUPS) * 3 * B_W
C_OFF = A_COLS + B_COLS
LORA_COLS = LORA_W + LORA_A + LORA_G
C_SHIFT_W = 3 * C_C + LORA_COLS
G_OFF = C_OFF + C_SHIFT_W

LANES = 128
WKV_CHUNK = 64
ROW_TILE = 512
VMEM_LIMIT = 56 * 1024 * 1024


def _cparams(*sem):
    return pltpu.CompilerParams(dimension_semantics=sem, vmem_limit_bytes=VMEM_LIMIT)


def _dot(a, b):
    return jnp.dot(a, b, preferred_element_type=F32)


def _dot_nt(a, b):
    return lax.dot_general(a, b, (((1,), (1,)), ((), ())), preferred_element_type=F32)


def _dot_tn(a, b):
    return lax.dot_general(a, b, (((0,), (0,)), ((), ())), preferred_element_type=F32)


def _sigmoid(x):
    return 1.0 / (1.0 + jnp.exp(-x))


def _silu(x):
    return x * _sigmoid(x)


def _layer_norm(y, g, b):
    mu = jnp.mean(y, axis=-1, keepdims=True)
    d = y - mu
    var = jnp.mean(d * d, axis=-1, keepdims=True)
    return d * lax.rsqrt(var + LN_EPS) * g + b


def _resident(shape):
    nd = len(shape)
    return pl.BlockSpec(shape, lambda *_: (0,) * nd, pipeline_mode=pl.Buffered(1))


def _mod_spec(mod, tm, tiles_per_seq):
    if mod.ndim == 4:
        k, _, _, d = mod.shape
        return pl.BlockSpec((k, None, 1, d), lambda i: (0, i // tiles_per_seq, 0, 0))
    k, _, d = mod.shape
    return pl.BlockSpec((k, tm, d), lambda i: (0, i, 0))


def _ada_kernel(c_ref, w_ref, b_ref, o_ref):
    c = c_ref[...]
    o_ref[...] = _dot(_silu(c).astype(BF16), w_ref[...].astype(BF16)) + b_ref[...]


def _ada(c_all, w_ada, b_ada):
    nl, d, nd = w_ada.shape
    nc = c_all.shape[0]
    tn = 1024
    return pl.pallas_call(
        _ada_kernel,
        grid=(nl, nd // tn),
        in_specs=[
            pl.BlockSpec((nc, d), lambda l, j: (0, 0)),
            pl.BlockSpec((None, d, tn), lambda l, j: (l, 0, j)),
            pl.BlockSpec((None, 1, tn), lambda l, j: (l, 0, j)),
        ],
        out_specs=pl.BlockSpec((None, nc, tn), lambda l, j: (l, 0, j)),
        out_shape=jax.ShapeDtypeStruct((nl, nc, nd), F32),
        compiler_params=_cparams("parallel", "parallel"),
        name="ada",
    )(c_all, w_ada, b_ada.reshape(nl, 1, nd))


def _ffn_kernel(x_ref, mod_ref, wup_ref, wdn_ref, lng_ref, lnb_ref, o_ref, acc_ref, *, fc):
    x = x_ref[...]
    u = (x * (1.0 + mod_ref[1]) + mod_ref[0]).astype(BF16)
    dff = wdn_ref.shape[0]
    for c in range(dff // fc):
        a = _dot(u, wup_ref[:, c * fc:(c + 1) * fc])
        b = _dot(u, wup_ref[:, dff + c * fc:dff + (c + 1) * fc])
        g = (_silu(a) * b).astype(BF16)
        d = _dot(g, wdn_ref[c * fc:(c + 1) * fc, :])
        if c == 0:
            acc_ref[...] = d
        else:
            acc_ref[...] += d
    y = ALPHA * x + 0.5 * (1.0 + mod_ref[2]) * acc_ref[...]
    o_ref[...] = _layer_norm(y, lng_ref[...], lnb_ref[...])


def _ffn(x, mod, w_up, w_dn, ln_g, ln_b, tiles_per_seq):
    r, d = x.shape
    dff = w_dn.shape[0]
    tm = min(ROW_TILE, r)
    return pl.pallas_call(
        functools.partial(_ffn_kernel, fc=256),
        grid=(r // tm,),
        in_specs=[
            pl.BlockSpec((tm, d), lambda i: (i, 0)),
            _mod_spec(mod, tm, tiles_per_seq),
            _resident((d, 2 * dff)),
            _resident((dff, d)),
            _resident((1, d)),
            _resident((1, d)),
        ],
        out_specs=pl.BlockSpec((tm, d), lambda i: (i, 0)),
        out_shape=jax.ShapeDtypeStruct((r, d), F32),
        scratch_shapes=[pltpu.VMEM((tm, d), F32)],
        compiler_params=_cparams("parallel"),
        name="ffn",
    )(x, mod, w_up, w_dn, ln_g, ln_b)


def _swap_halves(x):
    lane = lax.broadcasted_iota(jnp.int32, x.shape, 1)
    fwd = pltpu.roll(x, LANES - 32, 1)
    bwd = pltpu.roll(x, 32, 1)
    return jnp.where((lane % 64) < 32, fwd, bwd)


def _proj_kernel(x_ref, mod_ref, w_ref, cos_ref, sin_ref,
                 qa_ref, ka_ref, va_ref, ga_ref, b0_ref, b1_ref, b2_ref, prkv_ref, plora_ref):
    x = x_ref[...]
    u = (x * (1.0 + mod_ref[1]) + mod_ref[0]).astype(BF16)
    cos = cos_ref[...]
    sin = sin_ref[...]

    def cols(c0, c1):
        return _dot(u, w_ref[:, c0:c1])

    for ref, base, scale in ((qa_ref, 0, 1.0), (ka_ref, A_QK, DK_A ** -0.5)):
        z = cols(base, base + A_QK)
        for t in range(A_QK // LANES):
            zt = z[:, t * LANES:(t + 1) * LANES]
            rot = zt * cos + _swap_halves(zt) * sin
            ref[:, t * LANES:(t + 1) * LANES] = (rot * scale).astype(ref.dtype)
    va_ref[...] = cols(2 * A_QK, 2 * A_QK + A_V).astype(va_ref.dtype)
    ga_ref[...] = cols(2 * A_QK + A_V, A_COLS).astype(ga_ref.dtype)
    for gi, ref in enumerate((b0_ref, b1_ref, b2_ref)):
        base = A_COLS + gi * 3 * B_W
        ref[...] = cols(base, base + 3 * B_W).astype(ref.dtype)
    prkv_ref[...] = cols(C_OFF, C_OFF + 3 * C_C)
    plora_ref[...] = cols(C_OFF + 3 * C_C, G_OFF)


def _proj(x, mod, w_mix, cos_t, sin_t, tiles_per_seq, b_dtype):
    r, d = x.shape
    tm = min(ROW_TILE, r)
    row = lambda w: pl.BlockSpec((tm, w), lambda i: (i, 0))
    tab = pl.BlockSpec((tm, LANES), lambda i: (i % tiles_per_seq, 0))
    outs = [(A_QK, BF16), (A_QK, BF16), (A_V, BF16), (A_V, BF16),
            (3 * B_W, b_dtype), (3 * B_W, b_dtype), (3 * B_W, b_dtype),
            (3 * C_C, F32), (LORA_COLS, F32)]
    return pl.pallas_call(
        _proj_kernel,
        grid=(r // tm,),
        in_specs=[row(d), _mod_spec(mod, tm, tiles_per_seq), _resident((d, G_OFF)), tab, tab],
        out_specs=[row(w) for w, _ in outs],
        out_shape=[jax.ShapeDtypeStruct((r, w), dt) for w, dt in outs],
        compiler_params=_cparams("parallel"),
        name="proj",
    )(x, mod, w_mix, cos_t, sin_t)


def _ret_prompt_kernel(q_ref, k_ref, v_ref, g_ref, dmask_ref, qdec_ref, kdec_ref, gc_ref, bd_ref,
                       gng_ref, gnb_ref, y_ref, rfin_ref, st_ref, *, nchunk):
    i = pl.program_id(1)

    @pl.when(i == 0)
    def _():
        st_ref[...] = jnp.zeros_like(st_ref)

    c = CHUNK_A
    lane_head = lax.broadcasted_iota(jnp.int32, (1, A_QK), 1) // DK_A
    head_sel = [(lane_head == h).astype(F32).astype(BF16) for h in range(H_A)]
    for ci in range(nchunk):
        rows = slice(ci * c, (ci + 1) * c)
        q = q_ref[rows, :]
        k = k_ref[rows, :]
        v = v_ref[rows, :]
        st = st_ref[...]
        ys = []
        for h in range(H_A):
            att = _dot_nt(q * head_sel[h], k) * dmask_ref[h]
            ys.append(_dot(att.astype(BF16), v[:, h * DV_A:(h + 1) * DV_A]))
        y = jnp.concatenate(ys, axis=1)
        y = y + _dot((q.astype(F32) * qdec_ref[...]).astype(BF16), st.astype(BF16))
        kv = _dot_tn((k.astype(F32) * kdec_ref[...]).astype(BF16), v)
        st_ref[...] = st * gc_ref[...] + kv * bd_ref[...]
        outs = []
        for h in range(H_A):
            yh = y[:, h * DV_A:(h + 1) * DV_A]
            mu = jnp.mean(yh, axis=-1, keepdims=True)
            dd = yh - mu
            var = jnp.mean(dd * dd, axis=-1, keepdims=True)
            outs.append(dd * lax.rsqrt(var + GN_EPS_A))
        yn = jnp.concatenate(outs, axis=1) * gng_ref[...] + gnb_ref[...]
        y_ref[rows, :] = (yn * _silu(g_ref[rows, :].astype(F32))).astype(y_ref.dtype)

    @pl.when(i == pl.num_programs(1) - 1)
    def _():
        rfin_ref[...] = st_ref[...]


def _ret_tables(c):
    log_g = jnp.log(1.0 - 2.0 ** (-5.0 - jnp.arange(H_A, dtype=F32)))
    idx = jnp.arange(c, dtype=F32)
    diff = idx[:, None] - idx[None, :]
    dmask = jnp.where(diff >= 0, jnp.exp(jnp.maximum(diff, 0.0)[None] * log_g[:, None, None]), 0.0)
    kdec = jnp.exp((c - 1 - idx)[None, :] * log_g[:, None])
    qdec = jnp.exp((idx + 1)[None, :] * log_g[:, None])
    gc = jnp.exp(c * log_g)
    return dmask, kdec, qdec, gc


def _ret_prompt(qa, ka, va, ga, gn_g, gn_b, n, s):
    tr = min(ROW_TILE, s)
    nt = s // tr
    dmask, kdec, qdec, gc = _ret_tables(CHUNK_A)
    qdec_t = jnp.repeat(qdec.T, DK_A, axis=1)
    kdec_t = jnp.repeat(kdec.T, DK_A, axis=1)
    gc_t = jnp.broadcast_to(jnp.repeat(gc, DK_A)[:, None], (A_QK, A_V))
    bd = (jnp.arange(A_QK)[:, None] // DK_A == jnp.arange(A_V)[None, :] // DV_A).astype(F32)
    row = lambda w: pl.BlockSpec((tr, w), lambda b, i: (b * nt + i, 0))
    y, rfin = pl.pallas_call(
        functools.partial(_ret_prompt_kernel, nchunk=tr // CHUNK_A),
        grid=(n, nt),
        in_specs=[row(A_QK), row(A_QK), row(A_V), row(A_V),
                  _resident((H_A, CHUNK_A, CHUNK_A)), _resident((CHUNK_A, A_QK)),
                  _resident((CHUNK_A, A_QK)), _resident((A_QK, A_V)), _resident((A_QK, A_V)),
                  _resident((1, A_V)), _resident((1, A_V))],
        out_specs=[row(A_V), pl.BlockSpec((None, A_QK, A_V), lambda b, i: (b, 0, 0))],
        out_shape=[jax.ShapeDtypeStruct((n * s, A_V), BF16),
                   jax.ShapeDtypeStruct((n, A_QK, A_V), F32)],
        scratch_shapes=[pltpu.VMEM((A_QK, A_V), F32)],
        compiler_params=_cparams("parallel", "arbitrary"),
        name="ret_prompt",
    )(qa, ka, va, ga, dmask, qdec_t, kdec_t, gc_t, bd, gn_g, gn_b)
    r4 = rfin.reshape(n, H_A, DK_A, H_A, DV_A)
    r_fin = jnp.stack([r4[:, h, :, h, :] for h in range(H_A)], axis=1)
    return y, r_fin


def _dil_prompt_kernel(q_ref, kc_ref, vc_ref, kp_ref, vp_ref, o_ref, lse_ref, *, nband):
    i = pl.program_id(2)
    bb = BAND_B
    qi = lax.broadcasted_iota(jnp.int32, (bb, 2 * bb), 0)
    ki = lax.broadcasted_iota(jnp.int32, (bb, 2 * bb), 1)
    band = (ki >= qi) & (ki <= qi + bb)
    lane_head = lax.broadcasted_iota(jnp.int32, (1, B_W), 1) // DH_B
    head_sel = [(lane_head == h).astype(F32).astype(BF16) for h in range(H_B)]
    first_key = jnp.where(i > 0, 0, bb)
    for j in range(nband):
        q = q_ref[j * bb:(j + 1) * bb, :]
        if j == 0:
            k = jnp.concatenate([kp_ref[...], kc_ref[0:bb, :]], axis=0)
            v = jnp.concatenate([vp_ref[...], vc_ref[0:bb, :]], axis=0)
            valid = band & (ki >= first_key)
        else:
            k = kc_ref[(j - 1) * bb:(j + 1) * bb, :]
            v = vc_ref[(j - 1) * bb:(j + 1) * bb, :]
            valid = band
        o = jnp.zeros((bb, B_W), F32)
        lse_full = jnp.zeros((bb, B_W), F32)
        for h in range(H_B):
            sc = _dot_nt(q * head_sel[h], k) * DH_B ** -0.5
            sc = jnp.where(valid, sc, -jnp.inf)
            m = jnp.max(sc, axis=-1, keepdims=True)
            p = jnp.exp(sc - m)
            l = jnp.sum(p, axis=-1, keepdims=True)
            o = o + _dot((p / l).astype(BF16), v * head_sel[h])
            lse_full = lse_full + jnp.where(lane_head == h, m + jnp.log(l), 0.0)
        o_ref[j * bb:(j + 1) * bb, :] = o.astype(o_ref.dtype)
        lse_ref[j * bb:(j + 1) * bb, :] = lse_full


def _dil_prompt(qkv, n, s, dil):
    ls = s // dil
    tq = min(ROW_TILE, ls)
    nband = tq // BAND_B
    view = qkv.reshape(n, ls, dil * 3 * B_W)
    cur = lambda c: pl.BlockSpec((None, tq, B_W), lambda b, r, i: (b, i, 3 * r + c))
    prev = lambda c: pl.BlockSpec((None, BAND_B, B_W),
                                  lambda b, r, i: (b, jnp.maximum(i * nband - 1, 0), 3 * r + c))
    out = pl.BlockSpec((None, tq, B_W), lambda b, r, i: (b, i, r))
    o, lse = pl.pallas_call(
        functools.partial(_dil_prompt_kernel, nband=nband),
        grid=(n, dil, ls // tq),
        in_specs=[cur(0), cur(1), cur(2), prev(1), prev(2)],
        out_specs=[out, out],
        out_shape=[jax.ShapeDtypeStruct((n, ls, dil * B_W), BF16),
                   jax.ShapeDtypeStruct((n, ls, dil * B_W), F32)],
        compiler_params=_cparams("parallel", "parallel", "arbitrary"),
        name="dil_prompt",
    )(view, view, view, view, view)
    return o.reshape(n * s, B_W), lse.reshape(n * s, B_W)


def _head_sum(x, bones):
    hi = x.astype(BF16)
    lo = (x - hi.astype(F32)).astype(BF16)
    return _dot(hi, bones) + _dot(lo, bones)


def _softplus(x):
    return jnp.maximum(x, 0.0) + jnp.log(1.0 + jnp.exp(-jnp.abs(x)))


def _rwkv_prep(prkv, plora, prev_rkv, prev_lora, prm, v_first, bones):
    xs = prkv + (prev_rkv - prkv) * prm["mu_rkv"][...]
    xl = plora + (prev_lora - plora) * prm["mu_lora"][...]
    r = xs[:, 0:C_C]
    k = xs[:, C_C:2 * C_C]
    v0 = xs[:, 2 * C_C:3 * C_C]
    wlog = -_softplus(-(prm["w0"][...] + _dot(jnp.tanh(xl).astype(BF16), prm["w2"][...]))) - 0.5
    lw = -jnp.exp(wlog)
    a = _sigmoid(prm["a0"][...] + _dot(xl.astype(BF16), prm["a2"][...]))
    g = _dot(_sigmoid(xl).astype(BF16), prm["g2"][...])
    if v_first is None:
        v = v0
    else:
        mixv = _dot(_dot(v0.astype(BF16), prm["vr1"][...]).astype(BF16), prm["vr2"][...])
        v = v0 + (v_first - v0) * _sigmoid(prm["v0"][...] + mixv)
    kk = k * prm["k_k"][...]
    kk = kk / jnp.maximum(jnp.sqrt(_head_sum(kk * kk, bones)), 1e-12)
    kmod = k * (1.0 + (a - 1.0) * prm["k_a"][...])
    bonus = _head_sum(r * kmod * prm["r_k"][...], bones) * v
    return r, lw, kmod, v, -kk, kk * a, g, bonus, v0


def _rwkv_post(y, bonus, g, prm, bones):
    mu = _head_sum(y, bones) * (1.0 / N_C)
    d = y - mu
    var = _head_sum(d * d, bones) * (1.0 / N_C)
    yn = d * lax.rsqrt(var + GN_EPS_C) * prm["lnx_g"][...] + prm["lnx_b"][...]
    return (yn + bonus) * g


_RWKV_PARAM_NAMES = ("mu_rkv", "mu_lora", "w0", "w2", "a0", "a2", "g2", "k_k", "k_a", "r_k",
                     "lnx_g", "lnx_b", "v0", "vr1", "vr2")


def _shift_rows(x, first_row):
    rolled = pltpu.roll(x, 1, 0)
    row = lax.broadcasted_iota(jnp.int32, x.shape, 0)
    return jnp.where(row == 0, first_row, rolled)


NPAIR = C_C // LANES
WKV_SETUP_CHUNKS = 2


def _wkv_masks():
    c, hw = WKV_CHUNK, N_C
    ti = lax.broadcasted_iota(jnp.int32, (c, 2 * hw), 0)
    si = lax.broadcasted_iota(jnp.int32, (c, 2 * hw), 1) % hw
    bdm = (lax.broadcasted_iota(jnp.int32, (2 * c, 2 * hw), 0) // c
           == lax.broadcasted_iota(jnp.int32, (2 * c, 2 * hw), 1) // hw)
    return si < ti, si <= ti, (si == ti).astype(F32), bdm


def _block_diag(x, bdm):
    return jnp.where(bdm, jnp.concatenate([x, x], axis=0), 0.0).astype(BF16)


def _wkv_setup(chunks, ltri, r_ref, lw_ref, kmod_ref, v_ref, nkk_ref, beta_ref,
               lhs_s, uhat_s, arbk_s, bkd_s, ecl_s):
    c, hw = WKV_CHUNK, N_C
    strict, incl, ipair, bdm = _wkv_masks()
    bdm4 = jnp.concatenate([bdm, bdm], axis=0)
    probs = []
    for ci in chunks:
        rows = pl.ds(pl.multiple_of(ci * c, c), c)
        lw_all = lw_ref[rows, :]
        h1 = lw_all.astype(BF16)
        r1 = lw_all - h1.astype(F32)
        h2 = r1.astype(BF16)
        h3 = (r1 - h2.astype(F32)).astype(BF16)
        cs_all = _dot(ltri, h1) + _dot(ltri, h2) + _dot(ltri, h3)
        for p in range(NPAIR):
            sl = slice(p * LANES, (p + 1) * LANES)
            q = ci * NPAIR + p
            cs = cs_all[:, sl]
            cl = cs[c - 1:c, :]
            inv_p = jnp.exp(-cs)
            dec_end = jnp.exp(cl - cs)
            beta = beta_ref[rows, sl]
            kmod = kmod_ref[rows, sl]
            a_t = nkk_ref[rows, sl] * jnp.exp(cs - lw_all[:, sl])
            r_t = (r_ref[rows, sl] * jnp.exp(cs)).astype(BF16)
            b_t = beta * inv_p
            k_t = kmod * inv_p
            lhs_s[q, c:2 * c, :] = r_t
            bkd_s[q] = jnp.concatenate([beta * dec_end, kmod * dec_end], axis=0).astype(BF16)
            ecl_s[q] = jnp.exp(cl)
            x1 = jnp.concatenate([a_t.astype(BF16), r_t], axis=0)
            bk = jnp.where(bdm4, jnp.concatenate([b_t, b_t, k_t, k_t], axis=0), 0.0).astype(BF16)
            probs.append(dict(q=q, a_t=a_t, x1=x1, bk=bk, v=v_ref[rows, sl]))
    for d in probs:
        d["gm"] = _dot_nt(d.pop("x1"), d.pop("bk"))
    for d in probs:
        gm = d.pop("gm")
        d["a_ab"] = jnp.where(strict, gm[0:c, 0:2 * hw], 0.0)
        d["a_ak"] = jnp.where(strict, gm[0:c, 2 * hw:4 * hw], 0.0).astype(BF16)
        arbk_s[d["q"]] = jnp.concatenate([jnp.where(incl, gm[c:2 * c, 0:2 * hw], 0.0),
                                          jnp.where(incl, gm[c:2 * c, 2 * hw:4 * hw], 0.0)],
                                         axis=1).astype(BF16)
    for d in probs:
        a_ab = d.pop("a_ab")
        d["npow"] = _dot(a_ab.astype(BF16), _block_diag(a_ab, bdm))
        d["tm"] = ipair + a_ab
        d["w1"] = _dot(d.pop("a_ak"), _block_diag(d.pop("v"), bdm))
    for j in range(1, int(math.log2(c))):
        for d in probs:
            stacked = jnp.concatenate([d["npow"], d["tm"]], axis=0).astype(BF16)
            d["res"] = _dot(stacked, _block_diag(d["npow"], bdm))
        for d in probs:
            res = d.pop("res")
            d["tm"] = d["tm"] + res[c:2 * c, :]
            d["npow"] = res[0:c, :]
    for d in probs:
        rhs = jnp.concatenate([_block_diag(d.pop("a_t"), bdm), _block_diag(d.pop("w1"), bdm)], axis=1)
        d["res2"] = _dot(d.pop("tm").astype(BF16), rhs)
    for d in probs:
        res2 = d.pop("res2")
        lhs_s[d["q"], 0:c, :] = res2[:, 0:2 * hw].astype(BF16)
        uhat_s[d["q"]] = res2[:, 2 * hw:4 * hw]


def _wkv_scan(ci, s_ref, y_ref, v_ref, lhs_s, uhat_s, arbk_s, bkd_s, ecl_s):
    c = WKV_CHUNK
    _, _, _, bdm = _wkv_masks()
    rows = pl.ds(pl.multiple_of(ci * c, c), c)
    probs = []
    for p in range(NPAIR):
        q = ci * NPAIR + p
        s0 = s_ref[p]
        probs.append(dict(p=p, q=q, s0=s0, res3=_dot_nt(lhs_s[q], s0.astype(BF16))))
    for d in probs:
        sl = slice(d["p"] * LANES, (d["p"] + 1) * LANES)
        res3 = d.pop("res3")
        u = res3[0:c, :] + uhat_s[d["q"]]
        v = v_ref[rows, sl]
        d["y"] = res3[c:2 * c, :] + _dot(arbk_s[d["q"]],
                                         jnp.concatenate([_block_diag(u, bdm), _block_diag(v, bdm)], axis=0))
        d["upd"] = _dot_tn(jnp.concatenate([u, v], axis=0).astype(BF16), bkd_s[d["q"]])
    for d in probs:
        sl = slice(d["p"] * LANES, (d["p"] + 1) * LANES)
        y_ref[rows, sl] = d["y"]
        s_ref[d["p"]] = d["s0"] * ecl_s[d["q"]] + jnp.where(bdm, d["upd"], 0.0)


def _rwkv_prompt_kernel(*refs, has_vres, nchunk):
    it = iter(refs)
    prkv_ref, plora_ref = next(it), next(it)
    vfirst_ref = next(it) if has_vres else None
    bones_ref, ltri_ref = next(it), next(it)
    prm = {name: next(it) for name in _RWKV_PARAM_NAMES}
    yc_ref = next(it)
    vout_ref = None if has_vres else next(it)
    s_ref = next(it)
    carry_rkv, carry_lora = next(it), next(it)
    r_s, lw_s, kmod_s, v_s, nkk_s, beta_s, y_s = (next(it) for _ in range(7))
    lhs_s, uhat_s, arbk_s, bkd_s, ecl_s = (next(it) for _ in range(5))

    i = pl.program_id(1)

    @pl.when(i == 0)
    def _():
        carry_rkv[...] = jnp.zeros_like(carry_rkv)
        carry_lora[...] = jnp.zeros_like(carry_lora)
        s_ref[...] = jnp.zeros_like(s_ref)

    prkv = prkv_ref[...]
    plora = plora_ref[...]
    tr = prkv.shape[0]
    bones = bones_ref[...]
    r, lw, kmod, v, nkk, beta, g, bonus, v0 = _rwkv_prep(
        prkv, plora, _shift_rows(prkv, carry_rkv[...]), _shift_rows(plora, carry_lora[...]),
        prm, vfirst_ref[...] if has_vres else None, bones)
    carry_rkv[...] = prkv[tr - 1:tr, :]
    carry_lora[...] = plora[tr - 1:tr, :]
    r_s[...] = r
    lw_s[...] = lw
    kmod_s[...] = kmod
    v_s[...] = v
    nkk_s[...] = nkk
    beta_s[...] = beta
    if not has_vres:
        vout_ref[...] = v0
    ltri = ltri_ref[...]

    mid = (lhs_s, uhat_s, arbk_s, bkd_s, ecl_s)

    def setup(gi, carry):
        chunks = [gi * WKV_SETUP_CHUNKS + j for j in range(WKV_SETUP_CHUNKS)]
        _wkv_setup(chunks, ltri, r_s, lw_s, kmod_s, v_s, nkk_s, beta_s, *mid)
        return carry

    def scan(ci, carry):
        _wkv_scan(ci, s_ref, y_s, v_s, *mid)
        return carry

    lax.fori_loop(0, nchunk // WKV_SETUP_CHUNKS, setup, 0)
    lax.fori_loop(0, nchunk, scan, 0)
    yc_ref[...] = _rwkv_post(y_s[...], bonus, g, prm, bones).astype(yc_ref.dtype)


def _rwkv_consts():
    bones = (jnp.arange(C_C)[:, None] // N_C == jnp.arange(C_C)[None, :] // N_C).astype(BF16)
    ltri = (jnp.arange(WKV_CHUNK)[:, None] >= jnp.arange(WKV_CHUNK)[None, :]).astype(BF16)
    return bones, ltri


def _rwkv_prompt(prkv, plora, v_first, prm, n, s):
    tr = min(ROW_TILE, s)
    nt = s // tr
    has_vres = v_first is not None
    bones, ltri = _rwkv_consts()
    row = lambda w: pl.BlockSpec((tr, w), lambda b, i: (b * nt + i, 0))
    ins = [prkv, plora] + ([v_first] if has_vres else []) + [bones, ltri] + [prm[k] for k in _RWKV_PARAM_NAMES]
    in_specs = ([row(3 * C_C), row(LORA_COLS)] + ([row(C_C)] if has_vres else [])
                + [_resident(a.shape) for a in ins[(3 if has_vres else 2):]])
    npair = NPAIR
    nprob = (tr // WKV_CHUNK) * npair
    c = WKV_CHUNK
    st_spec = pl.BlockSpec((None, npair, LANES, LANES), lambda b, i: (b, 0, 0, 0))
    st_shape = jax.ShapeDtypeStruct((n, npair, LANES, LANES), F32)
    vout_spec = [] if has_vres else [row(C_C)]
    vout_shape = [] if has_vres else [jax.ShapeDtypeStruct((n * s, C_C), F32)]
    res = pl.pallas_call(
        functools.partial(_rwkv_prompt_kernel, has_vres=has_vres, nchunk=tr // WKV_CHUNK),
        grid=(n, nt),
        in_specs=in_specs,
        out_specs=[row(C_C)] + vout_spec + [st_spec],
        out_shape=[jax.ShapeDtypeStruct((n * s, C_C), BF16)] + vout_shape + [st_shape],
        scratch_shapes=[pltpu.VMEM((1, 3 * C_C), F32), pltpu.VMEM((1, LORA_COLS), F32)]
                       + [pltpu.VMEM((tr, C_C), F32) for _ in range(7)]
                       + [pltpu.VMEM((nprob, 2 * c, LANES), BF16), pltpu.VMEM((nprob, c, LANES), F32),
                          pltpu.VMEM((nprob, c, 2 * LANES), BF16), pltpu.VMEM((nprob, 2 * c, LANES), BF16),
                          pltpu.VMEM((nprob, 1, LANES), F32)],
        compiler_params=_cparams("parallel", "arbitrary"),
        name="rwkv_prompt",
    )(*ins)
    yc, s_bd = res[0], res[-1]
    vout = None if has_vres else res[1]
    s6 = s_bd.reshape(n, npair, 2, N_C, 2, N_C)
    s_fin = jnp.stack([s6[:, :, 0, :, 0, :], s6[:, :, 1, :, 1, :]], axis=2).reshape(n, H_C, N_C, N_C)
    return yc, vout, s_fin


def _mix_kernel(*refs, merged):
    x_ref, mod_ref, ya_ref = refs[0:3]
    nb_in = 1 if merged else 6
    b_refs = refs[3:3 + nb_in]
    (yc_ref, wg_ref, bg_ref, wa_ref, wb_ref, wc_ref, wo_ref, lng_ref, lnb_ref, out_ref) = refs[3 + nb_in:]
    x = x_ref[...]
    d = x.shape[1]
    u = (x * (1.0 + mod_ref[1]) + mod_ref[0]).astype(BF16)
    if merged:
        yb = b_refs[0][...]
    else:
        o0_ref, o1_ref, o2_ref, l0_ref, l1_ref, l2_ref = b_refs
        l0, l1, l2 = l0_ref[...], l1_ref[...], l2_ref[...]
        m = jnp.maximum(jnp.maximum(l0, l1), l2)
        e0, e1, e2 = jnp.exp(l0 - m), jnp.exp(l1 - m), jnp.exp(l2 - m)
        den = e0 + e1 + e2
        yb = ((e0 / den) * o0_ref[...].astype(F32) + (e1 / den) * o1_ref[...].astype(F32)
              + (e2 / den) * o2_ref[...].astype(F32))
    branches = (_dot(ya_ref[...].astype(BF16), wa_ref[...]),
                _dot(yb.astype(BF16), wb_ref[...]),
                _dot(yc_ref[...].astype(BF16), wc_ref[...]))
    mix = None
    for j, br in enumerate(branches):
        gate = _sigmoid(_dot(u, wg_ref[:, j * d:(j + 1) * d]) + bg_ref[:, j * d:(j + 1) * d])
        mix = gate * br if mix is None else mix + gate * br
    h = _dot(mix.astype(BF16), wo_ref[...])
    y = ALPHA * x + (1.0 + mod_ref[2]) * h
    out_ref[...] = _layer_norm(y, lng_ref[...], lnb_ref[...])


def _mix(x, mod, ya, yb_parts, yc, wg, bg, wa, wb, wc, wo, ln_g, ln_b, tiles_per_seq):
    r, d = x.shape
    tm = min(ROW_TILE, r)
    row = lambda w: pl.BlockSpec((tm, w), lambda i: (i, 0))
    weights = [wg, bg, wa, wb, wc, wo, ln_g, ln_b]
    ins = [x, mod, ya] + list(yb_parts) + [yc] + weights
    in_specs = ([row(d), _mod_spec(mod, tm, tiles_per_seq), row(A_V)] + [row(B_W)] * len(yb_parts)
                + [row(C_C)] + [_resident(a.shape) for a in weights])
    return pl.pallas_call(
        functools.partial(_mix_kernel, merged=len(yb_parts) == 1),
        grid=(r // tm,),
        in_specs=in_specs,
        out_specs=row(d),
        out_shape=jax.ShapeDtypeStruct((r, d), F32),
        compiler_params=_cparams("parallel"),
        name="mix",
    )(*ins)


RET_SAMPLE_SEQS = 4


def _ret_sample_kernel(q_ref, k_ref, v_ref, dm_ref, qd_ref, kd_ref, gc_ref, st_ref, y_ref, so_ref):
    probs = []
    for b in range(q_ref.shape[0]):
        for h in range(H_A):
            q, k, v = q_ref[b, h], k_ref[b, h], v_ref[b, h].astype(BF16)
            probs.append(dict(b=b, h=h, v=v, att=_dot_nt(q.astype(BF16), k.astype(BF16)) * dm_ref[h],
                              cross=_dot((q * qd_ref[h]).astype(BF16), st_ref[b, h].astype(BF16)),
                              kv=_dot_tn((k * kd_ref[h]).astype(BF16), v)))
    for d in probs:
        b, h = d["b"], d["h"]
        y_ref[b, h] = _dot(d["att"].astype(BF16), d["v"]) + d["cross"]
        so_ref[b, h] = st_ref[b, h] * gc_ref[h] + d["kv"]


def _ret_sample(q, k, v, states, layer, t):
    nb = q.shape[0]
    sb = RET_SAMPLE_SEQS
    dmask, kdec, qdec, gc = _ret_tables(t)
    padt = TOK_PAD - t
    dm = jnp.pad(dmask, ((0, 0), (0, padt), (0, padt)))
    qd = jnp.broadcast_to(jnp.pad(qdec, ((0, 0), (0, padt)))[:, :, None], (H_A, TOK_PAD, DK_A))
    kd = jnp.broadcast_to(jnp.pad(kdec, ((0, 0), (0, padt)))[:, :, None], (H_A, TOK_PAD, DK_A))
    gcb = jnp.broadcast_to(gc[:, None, None], (H_A, DK_A, DV_A))
    tok = lambda w: pl.BlockSpec((sb, H_A, TOK_PAD, w), lambda i: (i, 0, 0, 0))
    return pl.pallas_call(
        _ret_sample_kernel,
        grid=(nb // sb,),
        in_specs=[tok(DK_A), tok(DK_A), tok(DV_A), _resident(dm.shape), _resident(qd.shape),
                  _resident(kd.shape), _resident(gcb.shape),
                  pl.BlockSpec((None, sb, H_A, DK_A, DV_A), lambda i: (layer, i, 0, 0, 0))],
        out_specs=[tok(DV_A), pl.BlockSpec((sb, H_A, DK_A, DV_A), lambda i: (i, 0, 0, 0))],
        out_shape=[jax.ShapeDtypeStruct((nb, H_A, TOK_PAD, DV_A), F32),
                   jax.ShapeDtypeStruct(states.shape[1:], F32)],
        compiler_params=_cparams("parallel"),
        name="ret_sample",
    )(q, k, v, dm, qd, kd, gcb, states)


def _ret_norm_kernel(y_ref, g_ref, gng_ref, gnb_ref, o_ref):
    y = y_ref[...]
    outs = []
    for h in range(H_A):
        yh = y[:, h * DV_A:(h + 1) * DV_A]
        mu = jnp.mean(yh, axis=-1, keepdims=True)
        dd = yh - mu
        var = jnp.mean(dd * dd, axis=-1, keepdims=True)
        outs.append(dd * lax.rsqrt(var + GN_EPS_A))
    yn = jnp.concatenate(outs, axis=1) * gng_ref[...] + gnb_ref[...]
    o_ref[...] = (yn * _silu(g_ref[...].astype(F32))).astype(o_ref.dtype)


def _ret_norm(y, g, gn_g, gn_b):
    r = y.shape[0]
    full = lambda a: pl.BlockSpec(a.shape, lambda: (0,) * a.ndim)
    return pl.pallas_call(
        _ret_norm_kernel,
        in_specs=[full(y), full(g), full(gn_g), full(gn_b)],
        out_specs=pl.BlockSpec((r, A_V), lambda: (0, 0)),
        out_shape=jax.ShapeDtypeStruct((r, A_V), BF16),
        name="ret_norm",
    )(y, g, gn_g, gn_b)


TOK_PAD = 8


def _win_sample_kernel(*refs, t, has_prev):
    ng = len(DIL_GROUPS)
    q_refs, kn_refs, vn_refs = refs[0:ng], refs[ng:2 * ng], refs[2 * ng:3 * ng]
    kv_refs, w_refs = refs[3 * ng:4 * ng], refs[4 * ng:5 * ng]
    base = 5 * ng + (ng if has_prev else 0)
    yb_ref, o_refs = refs[base], refs[base + 1:base + 1 + ng]
    nrow = TOK_PAD
    ncol = lax.broadcasted_iota(jnp.int32, (nrow, nrow), 1)
    ntok = lax.broadcasted_iota(jnp.int32, (nrow, nrow), 0)
    probs = []
    for g, (_, dil) in enumerate(DIL_GROUPS):
        w = w_refs[g].shape[-1]
        lane = lax.broadcasted_iota(jnp.int32, (nrow, w), 1)
        tok = lax.broadcasted_iota(jnp.int32, (nrow, w), 0)
        valid = ((lane >= tok) if dil == 1 else (lane % dil == tok)) | (tok >= t)
        ok_new = (ncol <= ntok) if dil == 1 else (ncol == ntok)
        for h in range(H_B):
            q = q_refs[g][h].astype(BF16)
            probs.append(dict(g=g, h=h, valid=valid, ok_new=ok_new,
                              sc=_dot(q, w_refs[g][0, h].astype(BF16)),
                              sn=_dot_nt(q, kn_refs[g][h].astype(BF16))))
    for d in probs:
        sc = jnp.where(d.pop("valid"), d.pop("sc") * DH_B ** -0.5, -jnp.inf)
        sn = jnp.where(d.pop("ok_new"), d.pop("sn") * DH_B ** -0.5, -jnp.inf)
        m = jnp.maximum(jnp.max(sc, axis=-1, keepdims=True), jnp.max(sn, axis=-1, keepdims=True))
        p = jnp.exp(sc - m)
        pn = jnp.exp(sn - m)
        l = jnp.sum(p, axis=-1, keepdims=True) + jnp.sum(pn, axis=-1, keepdims=True)
        d["lse"] = m + jnp.log(l)
        d["l"] = l
        d["p"] = p.astype(BF16)
        d["pn"] = pn.astype(BF16)
    for d in probs:
        g, h = d["g"], d["h"]
        d["o"] = (_dot_nt(d.pop("p"), w_refs[g][1, h].astype(BF16))
                  + _dot(d.pop("pn"), vn_refs[g][h].astype(BF16))) / d.pop("l")
    for h in range(H_B):
        mine = [d for d in probs if d["h"] == h]
        m = functools.reduce(jnp.maximum, [d["lse"] for d in mine])
        es = [jnp.exp(d["lse"] - m) for d in mine]
        den = functools.reduce(lambda a, b: a + b, es)
        yb_ref[h] = functools.reduce(lambda a, b: a + b, [(e / den) * d["o"] for e, d in zip(es, mine)])
    sel = (lax.broadcasted_iota(jnp.int32, (nrow, LANES), 0) + (LANES - t)
           == lax.broadcasted_iota(jnp.int32, (nrow, LANES), 1)).astype(BF16)
    last = lax.broadcasted_iota(jnp.int32, (DH_B, LANES), 1) >= LANES - t
    for g in range(ng):
        w = w_refs[g].shape[-1]
        new = kv_refs[g][...]
        p1 = new.astype(BF16)
        r1 = new - p1.astype(F32)
        p2 = r1.astype(BF16)
        p3 = (r1 - p2.astype(F32)).astype(BF16)
        cols = _dot_tn(p1, sel) + _dot_tn(p2, sel) + _dot_tn(p3, sel)
        for kv in range(2):
            for h in range(H_B):
                rolled = pltpu.roll(w_refs[g][kv, h], w - t, 1)
                c0 = (kv * H_B + h) * DH_B
                if w > LANES:
                    o_refs[g][kv, h, :, 0:w - LANES] = rolled[:, 0:w - LANES]
                o_refs[g][kv, h, :, w - LANES:w] = jnp.where(last, cols[c0:c0 + DH_B, :],
                                                             rolled[:, w - LANES:w])


def _win_sample(qkv, wins_t, prev, layer, nb, t):
    ng = len(DIL_GROUPS)
    has_prev = prev is not None

    def heads(a):
        a = a.reshape(nb, t, H_B, DH_B).transpose(0, 2, 1, 3)
        return jnp.pad(a, ((0, 0), (0, 0), (0, TOK_PAD - t), (0, 0)))

    qs = [heads(x[:, 0:B_W]) for x in qkv]
    kns = [heads(x[:, B_W:2 * B_W]) for x in qkv]
    vns = [heads(x[:, 2 * B_W:3 * B_W]) for x in qkv]
    kvs = [jnp.pad(x[:, B_W:].reshape(nb, t, 2 * B_W), ((0, 0), (0, TOK_PAD - t), (0, 0))) for x in qkv]
    hspec = pl.BlockSpec((None, H_B, TOK_PAD, DH_B), lambda b: (b, 0, 0, 0))
    kvspec = pl.BlockSpec((None, TOK_PAD, 2 * B_W), lambda b: (b, 0, 0))
    wspec = lambda a: pl.BlockSpec((None, None) + a.shape[2:], lambda b: (layer, b, 0, 0, 0, 0))
    ins = qs + kns + vns + kvs + list(wins_t) + (list(prev) if has_prev else [])
    in_specs = ([hspec] * (3 * ng) + [kvspec] * ng + [wspec(a) for a in wins_t]
                + ([pl.BlockSpec(memory_space=pl.ANY)] * ng if has_prev else []))
    outs = pl.pallas_call(
        functools.partial(_win_sample_kernel, t=t, has_prev=has_prev),
        grid=(nb,),
        in_specs=in_specs,
        out_specs=[hspec] + [wspec(a) for a in wins_t],
        out_shape=[jax.ShapeDtypeStruct((nb, H_B, TOK_PAD, DH_B), F32)]
                  + [jax.ShapeDtypeStruct(a.shape, a.dtype) for a in wins_t],
        input_output_aliases={5 * ng + g: 1 + g for g in range(ng)} if has_prev else {},
        compiler_params=_cparams("parallel"),
        name="win_sample",
    )(*ins)
    yb = outs[0][:, :, :t].transpose(0, 2, 1, 3).reshape(nb * t, B_W)
    return yb, outs[1:]


def _rwkv_prep_kernel(*refs, has_vres):
    it = iter(refs)
    prkv_ref, plora_ref, prev_rkv_ref, prev_lora_ref = (next(it) for _ in range(4))
    vfirst_ref = next(it) if has_vres else None
    bones_ref = next(it)
    prm = {name: next(it) for name in _RWKV_PARAM_NAMES}
    outs = [next(it) for _ in range(9)]
    res = _rwkv_prep(prkv_ref[...], plora_ref[...], prev_rkv_ref[...], prev_lora_ref[...], prm,
                     vfirst_ref[...] if has_vres else None, bones_ref[...])
    r, lw, kmod, v, nkk, beta, g, bonus, v0 = res
    for ref, val in zip(outs, (r, jnp.exp(lw), kmod, v, nkk, beta, g, bonus, v0)):
        ref[...] = val


def _rwkv_prep_call(prkv, plora, prev_rkv, prev_lora, v_first, prm):
    has_vres = v_first is not None
    bones, _ = _rwkv_consts()
    ins = [prkv, plora, prev_rkv, prev_lora] + ([v_first] if has_vres else []) + [bones] + [prm[k] for k in _RWKV_PARAM_NAMES]
    full = lambda a: pl.BlockSpec(a.shape, lambda: (0,) * a.ndim)
    r = prkv.shape[0]
    return pl.pallas_call(
        functools.partial(_rwkv_prep_kernel, has_vres=has_vres),
        in_specs=[full(a) for a in ins],
        out_specs=[pl.BlockSpec((r, C_C), lambda: (0, 0))] * 9,
        out_shape=[jax.ShapeDtypeStruct((r, C_C), F32)] * 9,
        compiler_params=pltpu.CompilerParams(vmem_limit_bytes=VMEM_LIMIT),
        name="rwkv_prep",
    )(*ins)


def _wkv_sample_kernel(r_ref, w_ref, k_ref, v_ref, nkk_ref, beta_ref, st_ref, y_ref, so_ref, *, t):
    for ti in range(t):
        src = st_ref if ti == 0 else so_ref
        nkk_t, w_t, beta_t, k_t, r_t = nkk_ref[ti], w_ref[ti], beta_ref[ti], k_ref[ti], r_ref[ti]

        def step(vi, carry, src=src, ti=ti, nkk_t=nkk_t, w_t=w_t, beta_t=beta_t, k_t=k_t, r_t=r_t):
            s = src[vi]
            sa = jnp.sum(s * nkk_t, axis=0, keepdims=True)
            s_new = s * w_t + sa * beta_t + v_ref[ti, pl.ds(vi, 1), :] * k_t
            so_ref[vi] = s_new
            y_ref[ti, pl.ds(vi, 1), :] = jnp.sum(s_new * r_t, axis=0, keepdims=True)
            return carry

        lax.fori_loop(0, N_C, step, 0, unroll=4)


def _wkv_sample(r, w, k, v, nkk, beta, states, layer, t):
    nb = r.shape[-1]
    per_head = pl.BlockSpec((t, None, N_C, nb), lambda h: (0, h, 0, 0))
    return pl.pallas_call(
        functools.partial(_wkv_sample_kernel, t=t),
        grid=(H_C,),
        in_specs=[per_head] * 6 + [pl.BlockSpec((None, None, N_C, N_C, nb), lambda h: (layer, h, 0, 0, 0))],
        out_specs=[per_head, pl.BlockSpec((None, N_C, N_C, nb), lambda h: (h, 0, 0, 0))],
        out_shape=[jax.ShapeDtypeStruct((t, H_C, N_C, nb), F32),
                   jax.ShapeDtypeStruct(states.shape[1:], F32)],
        compiler_params=_cparams("parallel"),
        name="wkv_sample",
    )(r, w, k, v, nkk, beta, states)


def _rwkv_post_kernel(y_ref, bonus_ref, g_ref, bones_ref, lnxg_ref, lnxb_ref, o_ref):
    prm = {"lnx_g": lnxg_ref, "lnx_b": lnxb_ref}
    o_ref[...] = _rwkv_post(y_ref[...], bonus_ref[...], g_ref[...], prm, bones_ref[...]).astype(o_ref.dtype)


def _rwkv_post_call(y, bonus, g, prm):
    bones, _ = _rwkv_consts()
    ins = [y, bonus, g, bones, prm["lnx_g"], prm["lnx_b"]]
    full = lambda a: pl.BlockSpec(a.shape, lambda: (0,) * a.ndim)
    r = y.shape[0]
    return pl.pallas_call(
        _rwkv_post_kernel,
        in_specs=[full(a) for a in ins],
        out_specs=pl.BlockSpec((r, C_C), lambda: (0, 0)),
        out_shape=jax.ShapeDtypeStruct((r, C_C), BF16),
        name="rwkv_post",
    )(*ins)


def _rotary_tables(pos):
    half = DK_A // 2
    inv = 1.0 / (ROPE_BASE_A ** jnp.linspace(0.0, 1.0, half, dtype=F32))
    ang = pos.astype(F32)[:, None] * inv[None, :]
    cos, sin = jnp.cos(ang), jnp.sin(ang)
    return jnp.tile(cos, (1, 4)), jnp.tile(jnp.concatenate([-sin, sin], axis=1), (1, 2))


def _rwkv_params(l, w):
    row = lambda a: a.reshape(1, -1)
    pad_rows = lambda a, r0: jnp.zeros((LORA_COLS, a.shape[1]), BF16).at[r0:r0 + a.shape[0]].set(a.astype(BF16))
    prm = {
        "mu_rkv": row(w["rw_mu"][l, :3 * C_C]), "mu_lora": row(w["rw_mu"][l, 3 * C_C:]),
        "w0": row(w["rw_w0"][l]), "w2": pad_rows(w["rw_w2"][l], 0),
        "a0": row(w["rw_a0"][l]), "a2": pad_rows(w["rw_a2"][l], LORA_W),
        "g2": pad_rows(w["rw_g2"][l], LORA_W + LORA_A),
        "k_k": row(w["rw_kk"][l]), "k_a": row(w["rw_ka"][l]), "r_k": row(w["rw_rk"][l]),
        "lnx_g": row(w["rw_lnx_g"][l]), "lnx_b": row(w["rw_lnx_b"][l]),
    }
    lv = max(l - 1, 0)
    prm["v0"] = row(w["rw_v0"][lv])
    prm["vr1"] = w["rw_vr1"][lv].astype(BF16)
    prm["vr2"] = w["rw_vr2"][lv].astype(BF16)
    return prm


def _layer_weights(l, w):
    d = w["w_in"].shape[1]
    return dict(
        ffn1_up=w["ffn1_up"][l].astype(BF16), ffn1_dn=w["ffn1_down"][l].astype(BF16),
        ffn2_up=w["ffn2_up"][l].astype(BF16), ffn2_dn=w["ffn2_down"][l].astype(BF16),
        w_mix=w["w_in"][l, :, :G_OFF].astype(BF16), w_gate=w["w_in"][l, :, G_OFF:].astype(BF16),
        b_gate=w["b_gate"][l].reshape(1, -1),
        wa=w["w_proj_a"][l].astype(BF16), wb=w["w_proj_b"][l].astype(BF16),
        wc=w["w_proj_c"][l].astype(BF16), wo=w["w_out"][l].astype(BF16),
        ln_g=[w["ln_g"][l, j].reshape(1, d) for j in range(3)],
        ln_b=[w["ln_b"][l, j].reshape(1, d) for j in range(3)],
        gn_g=w["ret_gn_g"][l].reshape(1, -1), gn_b=w["ret_gn_b"][l].reshape(1, -1),
        rwkv=_rwkv_params(l, w),
    )


def _prompt_layer(x, m, lw, tabs, v_first, n, s):
    tps = s // min(ROW_TILE, s)
    x = _ffn(x, m[0:3], lw["ffn1_up"], lw["ffn1_dn"], lw["ln_g"][0], lw["ln_b"][0], tps)
    qa, ka, va, ga, b0, b1, b2, prkv, plora = _proj(x, m[3:6], lw["w_mix"], tabs[0], tabs[1], tps, BF16)
    ya, r_fin = _ret_prompt(qa, ka, va, ga, lw["gn_g"], lw["gn_b"], n, s)
    ob, lb, wins = [], [], []
    for qkv, (win, dil) in zip((b0, b1, b2), DIL_GROUPS):
        o, lse = _dil_prompt(qkv, n, s, dil)
        ob.append(o)
        lb.append(lse)
        wp = min(win, s)
        kv = qkv.reshape(n, s, 3, H_B, DH_B)[:, s - wp:, 1:3].astype(F32)
        wins.append(kv)
    yc, v_out, s_fin = _rwkv_prompt(prkv, plora, v_first, lw["rwkv"], n, s)
    if v_first is None:
        v_first = v_out
    sh_new = jnp.concatenate([prkv.reshape(n, s, -1)[:, -1], plora.reshape(n, s, -1)[:, -1]], axis=-1)
    x = _mix(x, m[3:6], ya, ob + lb, yc, lw["w_gate"], lw["b_gate"], lw["wa"], lw["wb"], lw["wc"], lw["wo"],
             lw["ln_g"][1], lw["ln_b"][1], tps)
    x = _ffn(x, m[6:9], lw["ffn2_up"], lw["ffn2_dn"], lw["ln_g"][2], lw["ln_b"][2], tps)
    return x, v_first, (r_fin, s_fin, sh_new, wins)


def _to_lanes(a, nb, t, heads):
    return a.reshape(nb, t, heads, -1).transpose(1, 2, 3, 0)


def _from_lanes(a):
    t, heads, w, nb = a.shape
    return a.transpose(3, 0, 1, 2).reshape(nb * t, heads * w)


def _token_heads(a, nb, t, heads):
    a = a.astype(F32).reshape(nb, t, heads, -1).transpose(0, 2, 1, 3)
    return jnp.pad(a, ((0, 0), (0, 0), (0, TOK_PAD - t), (0, 0)))


def _sample_layer(x, m, lw, tabs, v_first, l, past, new_wins, nb, t):
    state_ret, state_wkv_t, state_shift, wins_t = past
    x = _ffn(x, m[0:3], lw["ffn1_up"], lw["ffn1_dn"], lw["ln_g"][0], lw["ln_b"][0], 1)
    qa, ka, va, ga, b0, b1, b2, prkv, plora = _proj(x, m[3:6], lw["w_mix"], tabs[0], tabs[1], 1, F32)
    y_h, r_new = _ret_sample(_token_heads(qa, nb, t, H_A), _token_heads(ka, nb, t, H_A),
                             _token_heads(va, nb, t, H_A), state_ret, l, t)
    ya = _ret_norm(y_h[:, :, :t].transpose(0, 2, 1, 3).reshape(nb * t, A_V), ga, lw["gn_g"], lw["gn_b"])
    yb, new_wins = _win_sample((b0, b1, b2), wins_t, new_wins, l, nb, t)
    prev_rkv = jnp.concatenate([state_shift[l][:, None, :3 * C_C], prkv.reshape(nb, t, -1)[:, :-1]], axis=1)
    prev_lora = jnp.concatenate([state_shift[l][:, None, 3 * C_C:], plora.reshape(nb, t, -1)[:, :-1]], axis=1)
    r, dec, kmod, v, nkk, beta, g, bonus, v0 = _rwkv_prep_call(
        prkv, plora, prev_rkv.reshape(nb * t, -1), prev_lora.reshape(nb * t, -1), v_first, lw["rwkv"])
    if v_first is None:
        v_first = v0
    y_l, s_new_l = _wkv_sample(*[_to_lanes(a, nb, t, H_C) for a in (r, dec, kmod, v, nkk, beta)],
                               state_wkv_t, l, t)
    s_new = s_new_l.transpose(3, 0, 1, 2)
    yc = _rwkv_post_call(_from_lanes(y_l), bonus, g, lw["rwkv"])
    sh_new = jnp.concatenate([prkv.reshape(nb, t, -1)[:, -1], plora.reshape(nb, t, -1)[:, -1]], axis=-1)
    x = _mix(x, m[3:6], ya, [yb], yc, lw["w_gate"], lw["b_gate"], lw["wa"], lw["wb"], lw["wc"], lw["wo"],
             lw["ln_g"][1], lw["ln_b"][1], 1)
    x = _ffn(x, m[6:9], lw["ffn2_up"], lw["ffn2_dn"], lw["ln_g"][2], lw["ln_b"][2], 1)
    return x, v_first, (r_new, s_new, sh_new), new_wins


def kernel(x_prompt, x_sample, state_ret, state_wkv, state_shift, cache_win1, cache_win2, cache_win3, c_prompt, c_sample, w_ada, b_ada, ln_g, ln_b, ffn1_up, ffn1_down, ffn2_up, ffn2_down, w_in, b_gate, ret_gn_g, ret_gn_b, rw_mu, rw_w0, rw_w2, rw_a0, rw_a2, rw_g2, rw_kk, rw_ka, rw_rk, rw_lnx_g, rw_lnx_b, rw_v0, rw_vr1, rw_vr2, w_proj_a, w_proj_b, w_proj_c, w_out):
    w = dict(w_ada=w_ada, b_ada=b_ada, ln_g=ln_g, ln_b=ln_b, ffn1_up=ffn1_up, ffn1_down=ffn1_down,
             ffn2_up=ffn2_up, ffn2_down=ffn2_down, w_in=w_in, b_gate=b_gate, ret_gn_g=ret_gn_g,
             ret_gn_b=ret_gn_b, rw_mu=rw_mu, rw_w0=rw_w0, rw_w2=rw_w2, rw_a0=rw_a0, rw_a2=rw_a2,
             rw_g2=rw_g2, rw_kk=rw_kk, rw_ka=rw_ka, rw_rk=rw_rk, rw_lnx_g=rw_lnx_g, rw_lnx_b=rw_lnx_b,
             rw_v0=rw_v0, rw_vr1=rw_vr1, rw_vr2=rw_vr2, w_proj_a=w_proj_a, w_proj_b=w_proj_b,
             w_proj_c=w_proj_c, w_out=w_out)
    n, s, d = x_prompt.shape
    nb, t, _ = x_sample.shape
    depth = w_ada.shape[0]
    past_len = cache_win3.shape[2]

    m_all = _ada(jnp.concatenate([c_prompt, c_sample], axis=0), w_ada, b_ada).reshape(depth, n + nb, 9, d)
    tabs_p = _rotary_tables(jnp.arange(s))
    tabs_s = tuple(jnp.tile(tb, (nb, 1)) for tb in _rotary_tables(past_len + jnp.arange(t)))
    caches = (cache_win1, cache_win2, cache_win3)
    for cache, (win, _) in zip(caches, DIL_GROUPS):
        assert cache.shape[2] == win, "each window buffer must hold its whole window"
    wins_t = tuple(c.transpose(0, 1, 3, 4, 5, 2) for c in caches)
    past = (state_ret, state_wkv.transpose(0, 2, 3, 4, 1), state_shift, wins_t)

    xp = x_prompt.reshape(n * s, d)
    xs = x_sample.reshape(nb * t, d)
    vf_p = vf_s = None
    new_wins = None
    outs_p, outs_s = [], []
    for l in range(depth):
        lw = _layer_weights(l, w)
        m_p = m_all[l, :n].transpose(1, 0, 2)[:, :, None, :]
        m_s = jnp.repeat(m_all[l, n:], t, axis=0).transpose(1, 0, 2)
        xp, vf_p, st_p = _prompt_layer(xp, m_p, lw, tabs_p, vf_p, n, s)
        xs, vf_s, st_s, new_wins = _sample_layer(xs, m_s, lw, tabs_s, vf_s, l, past, new_wins, nb, t)
        outs_p.append(st_p)
        outs_s.append(st_s)

    stack = lambda outs, j: jnp.stack([o[j] for o in outs])
    p_wins = [jnp.stack([o[3][gi] for o in outs_p]) for gi in range(len(DIL_GROUPS))]
    s_wins = [a.transpose(0, 1, 5, 2, 3, 4) for a in new_wins]
    return (xp.reshape(n, s, d), xs.reshape(nb, t, d), stack(outs_p, 0), stack(outs_p, 1), stack(outs_p, 2),
            *p_wins, stack(outs_s, 0), stack(outs_s, 1), stack(outs_s, 2), *s_wins)
```

```python
import functools
import math

import numpy as np
import jax
import jax.numpy as jnp
from jax import lax
from jax.experimental import pallas as pl
from jax.experimental.pallas import tpu as pltpu

F32 = jnp.float32
BF16 = jnp.bfloat16

H_A, DK_A, DV_A = 4, 64, 128
CHUNK_A = 128
ROPE_BASE_A = 10000.0
GN_EPS_A = 1e-5
DIL_GROUPS = ((128, 1), (512, 4), (2048, 16))
H_B, DH_B = 4, 64
BAND_B = 128
H_C, N_C = 8, 64
C_C = H_C * N_C
LORA_W, LORA_A, LORA_G = 64, 64, 160
GN_EPS_C = 64e-5
LN_EPS = 1e-5
DEPTH = 4
ALPHA = (2 * DEPTH) ** 0.25
A_QK = H_A * DK_A
A_V = H_A * DV_A
A_COLS = 2 * A_QK + 2 * A_V
B_W = H_B * DH_B
B_COLS = len(DIL_GROUPS) * 3 * B_W
C_OFF = A_COLS + B_COLS
LORA_COLS = LORA_W + LORA_A + LORA_G
C_SHIFT_W = 3 * C_C + LORA_COLS
G_OFF = C_OFF + C_SHIFT_W

LANES = 128
WKV_CHUNK = 64
ROW_TILE = 512
VMEM_LIMIT = 56 * 1024 * 1024


def _cparams(*sem):
    return pltpu.CompilerParams(dimension_semantics=sem, vmem_limit_bytes=VMEM_LIMIT)


def _dot(a, b):
    return jnp.dot(a, b, preferred_element_type=F32)


def _dot_nt(a, b):
    return lax.dot_general(a, b, (((1,), (1,)), ((), ())), preferred_element_type=F32)


def _dot_tn(a, b):
    return lax.dot_general(a, b, (((0,), (0,)), ((), ())), preferred_element_type=F32)


def _sigmoid(x):
    return 1.0 / (1.0 + jnp.exp(-x))


def _silu(x):
    return x * _sigmoid(x)


def _layer_norm(y, g, b):
    mu = jnp.mean(y, axis=-1, keepdims=True)
    d = y - mu
    var = jnp.mean(d * d, axis=-1, keepdims=True)
    return d * lax.rsqrt(var + LN_EPS) * g + b


def _resident(shape):
    nd = len(shape)
    return pl.BlockSpec(shape, lambda *_: (0,) * nd, pipeline_mode=pl.Buffered(1))


def _mod_spec(mod, tm, tiles_per_seq):
    if mod.ndim == 4:
        k, _, _, d = mod.shape
        return pl.BlockSpec((k, None, 1, d), lambda i: (0, i // tiles_per_seq, 0, 0))
    k, _, d = mod.shape
    return pl.BlockSpec((k, tm, d), lambda i: (0, i, 0))


def _ada_kernel(c_ref, w_ref, b_ref, o_ref):
    c = c_ref[...]
    o_ref[...] = _dot(_silu(c).astype(BF16), w_ref[...].astype(BF16)) + b_ref[...]


def _ada(c_all, w_ada, b_ada):
    nl, d, nd = w_ada.shape
    nc = c_all.shape[0]
    tn = 1024
    return pl.pallas_call(
        _ada_kernel,
        grid=(nl, nd // tn),
        in_specs=[
            pl.BlockSpec((nc, d), lambda l, j: (0, 0)),
            pl.BlockSpec((None, d, tn), lambda l, j: (l, 0, j)),
            pl.BlockSpec((None, 1, tn), lambda l, j: (l, 0, j)),
        ],
        out_specs=pl.BlockSpec((None, nc, tn), lambda l, j: (l, 0, j)),
        out_shape=jax.ShapeDtypeStruct((nl, nc, nd), F32),
        compiler_params=_cparams("parallel", "parallel"),
        name="ada",
    )(c_all, w_ada, b_ada.reshape(nl, 1, nd))


def _ffn_kernel(x_ref, mod_ref, wup_ref, wdn_ref, lng_ref, lnb_ref, o_ref, acc_ref, *, fc):
    x = x_ref[...]
    u = (x * (1.0 + mod_ref[1]) + mod_ref[0]).astype(BF16)
    dff = wdn_ref.shape[0]
    for c in range(dff // fc):
        a = _dot(u, wup_ref[:, c * fc:(c + 1) * fc])
        b = _dot(u, wup_ref[:, dff + c * fc:dff + (c + 1) * fc])
        g = (_silu(a) * b).astype(BF16)
        d = _dot(g, wdn_ref[c * fc:(c + 1) * fc, :])
        if c == 0:
            acc_ref[...] = d
        else:
            acc_ref[...] += d
    y = ALPHA * x + 0.5 * (1.0 + mod_ref[2]) * acc_ref[...]
    o_ref[...] = _layer_norm(y, lng_ref[...], lnb_ref[...])


def _ffn(x, mod, w_up, w_dn, ln_g, ln_b, tiles_per_seq):
    r, d = x.shape
    dff = w_dn.shape[0]
    tm = min(ROW_TILE, r)
    return pl.pallas_call(
        functools.partial(_ffn_kernel, fc=256),
        grid=(r // tm,),
        in_specs=[
            pl.BlockSpec((tm, d), lambda i: (i, 0)),
            _mod_spec(mod, tm, tiles_per_seq),
            _resident((d, 2 * dff)),
            _resident((dff, d)),
            _resident((1, d)),
            _resident((1, d)),
        ],
        out_specs=pl.BlockSpec((tm, d), lambda i: (i, 0)),
        out_shape=jax.ShapeDtypeStruct((r, d), F32),
        scratch_shapes=[pltpu.VMEM((tm, d), F32)],
        compiler_params=_cparams("parallel"),
        name="ffn",
    )(x, mod, w_up, w_dn, ln_g, ln_b)


def _swap_halves(x):
    lane = lax.broadcasted_iota(jnp.int32, x.shape, 1)
    fwd = pltpu.roll(x, LANES - 32, 1)
    bwd = pltpu.roll(x, 32, 1)
    return jnp.where((lane % 64) < 32, fwd, bwd)


def _proj_kernel(x_ref, mod_ref, w_ref, cos_ref, sin_ref,
                 qa_ref, ka_ref, va_ref, ga_ref, b0_ref, b1_ref, b2_ref, prkv_ref, plora_ref):
    x = x_ref[...]
    u = (x * (1.0 + mod_ref[1]) + mod_ref[0]).astype(BF16)
    cos = cos_ref[...]
    sin = sin_ref[...]

    def cols(c0, c1):
        return _dot(u, w_ref[:, c0:c1])

    for ref, base, scale in ((qa_ref, 0, 1.0), (ka_ref, A_QK, DK_A ** -0.5)):
        z = cols(base, base + A_QK)
        for t in range(A_QK // LANES):
            zt = z[:, t * LANES:(t + 1) * LANES]
            rot = zt * cos + _swap_halves(zt) * sin
            ref[:, t * LANES:(t + 1) * LANES] = (rot * scale).astype(ref.dtype)
    va_ref[...] = cols(2 * A_QK, 2 * A_QK + A_V).astype(va_ref.dtype)
    ga_ref[...] = cols(2 * A_QK + A_V, A_COLS).astype(ga_ref.dtype)
    for gi, ref in enumerate((b0_ref, b1_ref, b2_ref)):
        base = A_COLS + gi * 3 * B_W
        z = cols(base, base + 3 * B_W)
        for c in range(3 * B_W // LANES):
            ref[c] = z[:, c * LANES:(c + 1) * LANES]
    prkv_ref[...] = cols(C_OFF, C_OFF + 3 * C_C)
    plora_ref[...] = cols(C_OFF + 3 * C_C, G_OFF)


def _proj(x, mod, w_mix, cos_t, sin_t, tiles_per_seq):
    r, d = x.shape
    tm = min(ROW_TILE, r)
    row = lambda w: pl.BlockSpec((tm, w), lambda i: (i, 0))
    tab = pl.BlockSpec((tm, LANES), lambda i: (i % tiles_per_seq, 0))
    rows_out = lambda w, dt: (row(w), jax.ShapeDtypeStruct((r, w), dt))
    tiles_out = (pl.BlockSpec((3 * B_TILES, tm, LANES), lambda i: (0, i, 0)),
                 jax.ShapeDtypeStruct((3 * B_TILES, r, LANES), F32))
    outs = [rows_out(A_QK, BF16), rows_out(A_QK, BF16), rows_out(A_V, BF16), rows_out(A_V, BF16),
            tiles_out, tiles_out, tiles_out, rows_out(3 * C_C, F32), rows_out(LORA_COLS, F32)]
    return pl.pallas_call(
        _proj_kernel,
        grid=(r // tm,),
        in_specs=[row(d), _mod_spec(mod, tm, tiles_per_seq), _resident((d, G_OFF)), tab, tab],
        out_specs=[spec for spec, _ in outs],
        out_shape=[shape for _, shape in outs],
        compiler_params=_cparams("parallel"),
        name="proj",
    )(x, mod, w_mix, cos_t, sin_t)


def _ret_prompt_kernel(q_ref, k_ref, v_ref, g_ref, dmask_ref, qdec_ref, kdec_ref, gc_ref, bd_ref,
                       gng_ref, gnb_ref, y_ref, rfin_ref, st_ref, *, nchunk):
    i = pl.program_id(1)

    @pl.when(i == 0)
    def _():
        st_ref[...] = jnp.zeros_like(st_ref)

    c = CHUNK_A
    lane_head = lax.broadcasted_iota(jnp.int32, (1, A_QK), 1) // DK_A
    head_sel = [(lane_head == h).astype(F32).astype(BF16) for h in range(H_A)]
    for ci in range(nchunk):
        rows = slice(ci * c, (ci + 1) * c)
        q = q_ref[rows, :]
        k = k_ref[rows, :]
        v = v_ref[rows, :]
        st = st_ref[...]
        ys = []
        for h in range(H_A):
            att = _dot_nt(q * head_sel[h], k) * dmask_ref[h]
            ys.append(_dot(att.astype(BF16), v[:, h * DV_A:(h + 1) * DV_A]))
        y = jnp.concatenate(ys, axis=1)
        y = y + _dot((q.astype(F32) * qdec_ref[...]).astype(BF16), st.astype(BF16))
        kv = _dot_tn((k.astype(F32) * kdec_ref[...]).astype(BF16), v)
        st_ref[...] = st * gc_ref[...] + kv * bd_ref[...]
        outs = []
        for h in range(H_A):
            yh = y[:, h * DV_A:(h + 1) * DV_A]
            mu = jnp.mean(yh, axis=-1, keepdims=True)
            dd = yh - mu
            var = jnp.mean(dd * dd, axis=-1, keepdims=True)
            outs.append(dd * lax.rsqrt(var + GN_EPS_A))
        yn = jnp.concatenate(outs, axis=1) * gng_ref[...] + gnb_ref[...]
        y_ref[rows, :] = (yn * _silu(g_ref[rows, :].astype(F32))).astype(y_ref.dtype)

    @pl.when(i == pl.num_programs(1) - 1)
    def _():
        rfin_ref[...] = st_ref[...]


def _ret_tables(c):
    log_g = jnp.log(1.0 - 2.0 ** (-5.0 - jnp.arange(H_A, dtype=F32)))
    idx = jnp.arange(c, dtype=F32)
    diff = idx[:, None] - idx[None, :]
    dmask = jnp.where(diff >= 0, jnp.exp(jnp.maximum(diff, 0.0)[None] * log_g[:, None, None]), 0.0)
    kdec = jnp.exp((c - 1 - idx)[None, :] * log_g[:, None])
    qdec = jnp.exp((idx + 1)[None, :] * log_g[:, None])
    gc = jnp.exp(c * log_g)
    return dmask, kdec, qdec, gc


def _ret_prompt(qa, ka, va, ga, gn_g, gn_b, n, s):
    tr = min(ROW_TILE, s)
    nt = s // tr
    dmask, kdec, qdec, gc = _ret_tables(CHUNK_A)
    qdec_t = jnp.repeat(qdec.T, DK_A, axis=1)
    kdec_t = jnp.repeat(kdec.T, DK_A, axis=1)
    gc_t = jnp.broadcast_to(jnp.repeat(gc, DK_A)[:, None], (A_QK, A_V))
    bd = (jnp.arange(A_QK)[:, None] // DK_A == jnp.arange(A_V)[None, :] // DV_A).astype(F32)
    row = lambda w: pl.BlockSpec((tr, w), lambda b, i: (b * nt + i, 0))
    y, rfin = pl.pallas_call(
        functools.partial(_ret_prompt_kernel, nchunk=tr // CHUNK_A),
        grid=(n, nt),
        in_specs=[row(A_QK), row(A_QK), row(A_V), row(A_V),
                  _resident((H_A, CHUNK_A, CHUNK_A)), _resident((CHUNK_A, A_QK)),
                  _resident((CHUNK_A, A_QK)), _resident((A_QK, A_V)), _resident((A_QK, A_V)),
                  _resident((1, A_V)), _resident((1, A_V))],
        out_specs=[row(A_V), pl.BlockSpec((None, A_QK, A_V), lambda b, i: (b, 0, 0))],
        out_shape=[jax.ShapeDtypeStruct((n * s, A_V), BF16),
                   jax.ShapeDtypeStruct((n, A_QK, A_V), F32)],
        scratch_shapes=[pltpu.VMEM((A_QK, A_V), F32)],
        compiler_params=_cparams("parallel", "arbitrary"),
        name="ret_prompt",
    )(qa, ka, va, ga, dmask, qdec_t, kdec_t, gc_t, bd, gn_g, gn_b)
    r4 = rfin.reshape(n, H_A, DK_A, H_A, DV_A)
    r_fin = jnp.stack([r4[:, h, :, h, :] for h in range(H_A)], axis=1)
    return y, r_fin


B_TILES = B_W // LANES


def _lane_tiles(ref, rows, first):
    return jnp.concatenate([ref[first + c, rows, :] for c in range(B_TILES)], axis=1)


def _dil_prompt_kernel(cur_ref, prev_ref, o_ref, lse_ref, *, dil, nband):
    i = pl.program_id(1)
    bb = BAND_B
    qi = lax.broadcasted_iota(jnp.int32, (bb, 2 * bb), 0)
    ki = lax.broadcasted_iota(jnp.int32, (bb, 2 * bb), 1)
    band = (ki >= qi) & (ki <= qi + bb)
    lane_head = lax.broadcasted_iota(jnp.int32, (1, B_W), 1) // DH_B
    head_sel = [(lane_head == h).astype(F32).astype(BF16) for h in range(H_B)]
    first_key = jnp.where(i > 0, 0, bb)

    def rows_of(r, j):
        start = r + dil * bb * j
        return pl.ds(start, bb) if dil == 1 else pl.ds(start, bb, stride=dil)

    def residue(r):
        for j in range(nband):
            rows = rows_of(r, j)
            src, prows = (prev_ref, rows_of(r, 0)) if j == 0 else (cur_ref, rows_of(r, j - 1))
            valid = band & (ki >= first_key) if j == 0 else band
            q = _lane_tiles(cur_ref, rows, 0).astype(BF16)
            k = jnp.concatenate([_lane_tiles(src, prows, B_TILES), _lane_tiles(cur_ref, rows, B_TILES)],
                                axis=0).astype(BF16)
            v = jnp.concatenate([_lane_tiles(src, prows, 2 * B_TILES),
                                 _lane_tiles(cur_ref, rows, 2 * B_TILES)], axis=0).astype(BF16)
            scs = [_dot_nt(q * head_sel[h], k) for h in range(H_B)]
            o = jnp.zeros((bb, B_W), F32)
            lse_full = jnp.zeros((bb, B_W), F32)
            for h in range(H_B):
                sc = jnp.where(valid, scs[h] * DH_B ** -0.5, -jnp.inf)
                m = jnp.max(sc, axis=-1, keepdims=True)
                p = jnp.exp(sc - m)
                l = jnp.sum(p, axis=-1, keepdims=True)
                o = o + _dot((p / l).astype(BF16), v * head_sel[h])
                lse_full = lse_full + jnp.where(lane_head == h, m + jnp.log(l), 0.0)
            for c in range(B_TILES):
                o_ref[c, rows, :] = o[:, c * LANES:(c + 1) * LANES]
                lse_ref[c, rows, :] = lse_full[:, c * LANES:(c + 1) * LANES]

    if dil == 1:
        residue(0)
    else:
        def body(r, carry):
            residue(r)
            return carry

        lax.fori_loop(0, dil, body, 0, unroll=2)


def _dil_prompt(qkv, n, s, dil):
    prev_rows = BAND_B * dil
    nband = max(1, ROW_TILE // prev_rows)
    st = prev_rows * nband
    nst = s // st
    cur = pl.BlockSpec((3 * B_TILES, st, LANES), lambda b, i: (0, b * nst + i, 0))
    prev = pl.BlockSpec((3 * B_TILES, prev_rows, LANES),
                        lambda b, i: (0, jnp.maximum((b * nst + i) * nband - 1, 0), 0))
    out = pl.BlockSpec((B_TILES, st, LANES), lambda b, i: (0, b * nst + i, 0))
    return pl.pallas_call(
        functools.partial(_dil_prompt_kernel, dil=dil, nband=nband),
        grid=(n, nst),
        in_specs=[cur, prev],
        out_specs=[out, out],
        out_shape=[jax.ShapeDtypeStruct((B_TILES, n * s, LANES), F32)] * 2,
        compiler_params=_cparams("parallel", "arbitrary"),
        name="dil_prompt",
    )(qkv, qkv)


def _head_sum(x, bones):
    hi = x.astype(BF16)
    lo = (x - hi.astype(F32)).astype(BF16)
    return _dot(hi, bones) + _dot(lo, bones)


def _softplus(x):
    return jnp.maximum(x, 0.0) + jnp.log(1.0 + jnp.exp(-jnp.abs(x)))


def _rwkv_prep(prkv, plora, prev_rkv, prev_lora, prm, v_first, bones):
    xs = prkv + (prev_rkv - prkv) * prm["mu_rkv"][...]
    xl = plora + (prev_lora - plora) * prm["mu_lora"][...]
    r = xs[:, 0:C_C]
    k = xs[:, C_C:2 * C_C]
    v0 = xs[:, 2 * C_C:3 * C_C]
    wlog = -_softplus(-(prm["w0"][...] + _dot(jnp.tanh(xl).astype(BF16), prm["w2"][...]))) - 0.5
    lw = -jnp.exp(wlog)
    a = _sigmoid(prm["a0"][...] + _dot(xl.astype(BF16), prm["a2"][...]))
    g = _dot(_sigmoid(xl).astype(BF16), prm["g2"][...])
    if v_first is None:
        v = v0
    else:
        mixv = _dot(_dot(v0.astype(BF16), prm["vr1"][...]).astype(BF16), prm["vr2"][...])
        v = v0 + (v_first - v0) * _sigmoid(prm["v0"][...] + mixv)
    kk = k * prm["k_k"][...]
    kk = kk / jnp.maximum(jnp.sqrt(_head_sum(kk * kk, bones)), 1e-12)
    kmod = k * (1.0 + (a - 1.0) * prm["k_a"][...])
    bonus = _head_sum(r * kmod * prm["r_k"][...], bones) * v
    return r, lw, kmod, v, -kk, kk * a, g, bonus, v0


def _rwkv_post(y, bonus, g, prm, bones):
    mu = _head_sum(y, bones) * (1.0 / N_C)
    d = y - mu
    var = _head_sum(d * d, bones) * (1.0 / N_C)
    yn = d * lax.rsqrt(var + GN_EPS_C) * prm["lnx_g"][...] + prm["lnx_b"][...]
    return (yn + bonus) * g


_RWKV_PARAM_NAMES = ("mu_rkv", "mu_lora", "w0", "w2", "a0", "a2", "g2", "k_k", "k_a", "r_k",
                     "lnx_g", "lnx_b", "v0", "vr1", "vr2")


def _shift_rows(x, first_row):
    rolled = pltpu.roll(x, 1, 0)
    row = lax.broadcasted_iota(jnp.int32, x.shape, 0)
    return jnp.where(row == 0, first_row, rolled)


NPAIR = C_C // LANES
WKV_SETUP_CHUNKS = 4


def _wkv_masks():
    c, hw = WKV_CHUNK, N_C
    ti = lax.broadcasted_iota(jnp.int32, (c, 2 * hw), 0)
    si = lax.broadcasted_iota(jnp.int32, (c, 2 * hw), 1) % hw
    bdm = (lax.broadcasted_iota(jnp.int32, (2 * c, 2 * hw), 0) // c
           == lax.broadcasted_iota(jnp.int32, (2 * c, 2 * hw), 1) // hw)
    return si < ti, si <= ti, (si == ti).astype(F32), bdm


def _block_diag(x, bdm):
    return jnp.where(bdm, jnp.concatenate([x, x], axis=0), 0.0).astype(BF16)


def _wkv_setup(chunks, ltri, r_ref, lw_ref, kmod_ref, v_ref, nkk_ref, beta_ref,
               lhs_s, uhat_s, arbk_s, bkd_s, ecl_s):
    c, hw = WKV_CHUNK, N_C
    strict, incl, ipair, bdm = _wkv_masks()
    bdm4 = jnp.concatenate([bdm, bdm], axis=0)
    probs = []
    for ci in chunks:
        rows = pl.ds(pl.multiple_of(ci * c, c), c)
        lw_all = lw_ref[rows, :]
        h1 = lw_all.astype(BF16)
        r1 = lw_all - h1.astype(F32)
        h2 = r1.astype(BF16)
        h3 = (r1 - h2.astype(F32)).astype(BF16)
        cs_all = _dot(ltri, h1) + _dot(ltri, h2) + _dot(ltri, h3)
        for p in range(NPAIR):
            sl = slice(p * LANES, (p + 1) * LANES)
            q = ci * NPAIR + p
            cs = cs_all[:, sl]
            cl = cs[c - 1:c, :]
            inv_p = jnp.exp(-cs)
            dec_end = jnp.exp(cl - cs)
            beta = beta_ref[rows, sl]
            kmod = kmod_ref[rows, sl]
            a_t = nkk_ref[rows, sl] * jnp.exp(cs - lw_all[:, sl])
            r_t = (r_ref[rows, sl] * jnp.exp(cs)).astype(BF16)
            b_t = beta * inv_p
            k_t = kmod * inv_p
            lhs_s[q, c:2 * c, :] = r_t
            bkd_s[q] = jnp.concatenate([beta * dec_end, kmod * dec_end], axis=0).astype(BF16)
            ecl_s[q] = jnp.exp(cl)
            x1 = jnp.concatenate([a_t.astype(BF16), r_t], axis=0)
            bk = jnp.where(bdm4, jnp.concatenate([b_t, b_t, k_t, k_t], axis=0), 0.0).astype(BF16)
            probs.append(dict(q=q, a_t=a_t, x1=x1, bk=bk, v=v_ref[rows, sl]))
    for d in probs:
        d["gm"] = _dot_nt(d.pop("x1"), d.pop("bk"))
    for d in probs:
        gm = d.pop("gm")
        d["a_ab"] = jnp.where(strict, gm[0:c, 0:2 * hw], 0.0)
        d["a_ak"] = jnp.where(strict, gm[0:c, 2 * hw:4 * hw], 0.0).astype(BF16)
        arbk_s[d["q"]] = jnp.concatenate([jnp.where(incl, gm[c:2 * c, 0:2 * hw], 0.0),
                                          jnp.where(incl, gm[c:2 * c, 2 * hw:4 * hw], 0.0)],
                                         axis=1).astype(BF16)
    for d in probs:
        a_ab = d.pop("a_ab")
        d["npow"] = _dot(a_ab.astype(BF16), _block_diag(a_ab, bdm))
        d["tm"] = ipair + a_ab
        d["w1"] = _dot(d.pop("a_ak"), _block_diag(d.pop("v"), bdm))
    for j in range(1, int(math.log2(c))):
        for d in probs:
            stacked = jnp.concatenate([d["npow"], d["tm"]], axis=0).astype(BF16)
            d["res"] = _dot(stacked, _block_diag(d["npow"], bdm))
        for d in probs:
            res = d.pop("res")
            d["tm"] = d["tm"] + res[c:2 * c, :]
            d["npow"] = res[0:c, :]
    for d in probs:
        rhs = jnp.concatenate([_block_diag(d.pop("a_t"), bdm), _block_diag(d.pop("w1"), bdm)], axis=1)
        d["res2"] = _dot(d.pop("tm").astype(BF16), rhs)
    for d in probs:
        res2 = d.pop("res2")
        lhs_s[d["q"], 0:c, :] = res2[:, 0:2 * hw].astype(BF16)
        uhat_s[d["q"]] = res2[:, 2 * hw:4 * hw]


def _wkv_scan(ci, s_ref, y_ref, v_ref, lhs_s, uhat_s, arbk_s, bkd_s, ecl_s):
    c = WKV_CHUNK
    _, _, _, bdm = _wkv_masks()
    rows = pl.ds(pl.multiple_of(ci * c, c), c)
    probs = []
    for p in range(NPAIR):
        q = ci * NPAIR + p
        s0 = s_ref[p]
        probs.append(dict(p=p, q=q, s0=s0, res3=_dot_nt(lhs_s[q], s0.astype(BF16))))
    for d in probs:
        sl = slice(d["p"] * LANES, (d["p"] + 1) * LANES)
        res3 = d.pop("res3")
        u = res3[0:c, :] + uhat_s[d["q"]]
        v = v_ref[rows, sl]
        d["y"] = res3[c:2 * c, :] + _dot(arbk_s[d["q"]],
                                         jnp.concatenate([_block_diag(u, bdm), _block_diag(v, bdm)], axis=0))
        d["upd"] = _dot_tn(jnp.concatenate([u, v], axis=0).astype(BF16), bkd_s[d["q"]])
    for d in probs:
        sl = slice(d["p"] * LANES, (d["p"] + 1) * LANES)
        y_ref[rows, sl] = d["y"]
        s_ref[d["p"]] = d["s0"] * ecl_s[d["q"]] + jnp.where(bdm, d["upd"], 0.0)


def _rwkv_prompt_kernel(*refs, has_vres, nchunk):
    it = iter(refs)
    prkv_ref, plora_ref = next(it), next(it)
    vfirst_ref = next(it) if has_vres else None
    bones_ref, ltri_ref = next(it), next(it)
    prm = {name: next(it) for name in _RWKV_PARAM_NAMES}
    yc_ref = next(it)
    vout_ref = None if has_vres else next(it)
    s_ref = next(it)
    carry_rkv, carry_lora = next(it), next(it)
    r_s, lw_s, kmod_s, v_s, nkk_s, beta_s, y_s = (next(it) for _ in range(7))
    lhs_s, uhat_s, arbk_s, bkd_s, ecl_s = (next(it) for _ in range(5))

    i = pl.program_id(1)

    @pl.when(i == 0)
    def _():
        carry_rkv[...] = jnp.zeros_like(carry_rkv)
        carry_lora[...] = jnp.zeros_like(carry_lora)
        s_ref[...] = jnp.zeros_like(s_ref)

    prkv = prkv_ref[...]
    plora = plora_ref[...]
    tr = prkv.shape[0]
    bones = bones_ref[...]
    r, lw, kmod, v, nkk, beta, g, bonus, v0 = _rwkv_prep(
        prkv, plora, _shift_rows(prkv, carry_rkv[...]), _shift_rows(plora, carry_lora[...]),
        prm, vfirst_ref[...] if has_vres else None, bones)
    carry_rkv[...] = prkv[tr - 1:tr, :]
    carry_lora[...] = plora[tr - 1:tr, :]
    r_s[...] = r
    lw_s[...] = lw
    kmod_s[...] = kmod
    v_s[...] = v
    nkk_s[...] = nkk
    beta_s[...] = beta
    if not has_vres:
        vout_ref[...] = v0
    ltri = ltri_ref[...]

    mid = (lhs_s, uhat_s, arbk_s, bkd_s, ecl_s)

    def setup(gi, carry):
        chunks = [gi * WKV_SETUP_CHUNKS + j for j in range(WKV_SETUP_CHUNKS)]
        _wkv_setup(chunks, ltri, r_s, lw_s, kmod_s, v_s, nkk_s, beta_s, *mid)
        return carry

    def scan(ci, carry):
        _wkv_scan(ci, s_ref, y_s, v_s, *mid)
        return carry

    lax.fori_loop(0, nchunk // WKV_SETUP_CHUNKS, setup, 0)
    lax.fori_loop(0, nchunk, scan, 0)
    yc_ref[...] = _rwkv_post(y_s[...], bonus, g, prm, bones).astype(yc_ref.dtype)


def _rwkv_consts():
    bones = (jnp.arange(C_C)[:, None] // N_C == jnp.arange(C_C)[None, :] // N_C).astype(BF16)
    ltri = (jnp.arange(WKV_CHUNK)[:, None] >= jnp.arange(WKV_CHUNK)[None, :]).astype(BF16)
    return bones, ltri


def _rwkv_prompt(prkv, plora, v_first, prm, n, s):
    tr = min(ROW_TILE, s)
    nt = s // tr
    has_vres = v_first is not None
    bones, ltri = _rwkv_consts()
    row = lambda w: pl.BlockSpec((tr, w), lambda b, i: (b * nt + i, 0))
    ins = [prkv, plora] + ([v_first] if has_vres else []) + [bones, ltri] + [prm[k] for k in _RWKV_PARAM_NAMES]
    in_specs = ([row(3 * C_C), row(LORA_COLS)] + ([row(C_C)] if has_vres else [])
                + [_resident(a.shape) for a in ins[(3 if has_vres else 2):]])
    npair = NPAIR
    nprob = (tr // WKV_CHUNK) * npair
    c = WKV_CHUNK
    st_spec = pl.BlockSpec((None, npair, LANES, LANES), lambda b, i: (b, 0, 0, 0))
    st_shape = jax.ShapeDtypeStruct((n, npair, LANES, LANES), F32)
    vout_spec = [] if has_vres else [row(C_C)]
    vout_shape = [] if has_vres else [jax.ShapeDtypeStruct((n * s, C_C), F32)]
    res = pl.pallas_call(
        functools.partial(_rwkv_prompt_kernel, has_vres=has_vres, nchunk=tr // WKV_CHUNK),
        grid=(n, nt),
        in_specs=in_specs,
        out_specs=[row(C_C)] + vout_spec + [st_spec],
        out_shape=[jax.ShapeDtypeStruct((n * s, C_C), BF16)] + vout_shape + [st_shape],
        scratch_shapes=[pltpu.VMEM((1, 3 * C_C), F32), pltpu.VMEM((1, LORA_COLS), F32)]
                       + [pltpu.VMEM((tr, C_C), F32) for _ in range(7)]
                       + [pltpu.VMEM((nprob, 2 * c, LANES), BF16), pltpu.VMEM((nprob, c, LANES), F32),
                          pltpu.VMEM((nprob, c, 2 * LANES), BF16), pltpu.VMEM((nprob, 2 * c, LANES), BF16),
                          pltpu.VMEM((nprob, 1, LANES), F32)],
        compiler_params=_cparams("parallel", "arbitrary"),
        name="rwkv_prompt",
    )(*ins)
    yc, s_bd = res[0], res[-1]
    vout = None if has_vres else res[1]
    s6 = s_bd.reshape(n, npair, 2, N_C, 2, N_C)
    s_fin = jnp.stack([s6[:, :, 0, :, 0, :], s6[:, :, 1, :, 1, :]], axis=2).reshape(n, H_C, N_C, N_C)
    return yc, vout, s_fin


def _mix_kernel(*refs, merged):
    x_ref, mod_ref, ya_ref = refs[0:3]
    nb_in = 1 if merged else 6
    b_refs = refs[3:3 + nb_in]
    (yc_ref, wg_ref, bg_ref, wa_ref, wb_ref, wc_ref, wo_ref, lng_ref, lnb_ref, out_ref) = refs[3 + nb_in:]
    x = x_ref[...]
    d = x.shape[1]
    u = (x * (1.0 + mod_ref[1]) + mod_ref[0]).astype(BF16)
    if merged:
        yb = b_refs[0][...]
    else:
        rows = slice(None)
        o0, o1, o2, l0, l1, l2 = (_lane_tiles(ref, rows, 0) for ref in b_refs)
        m = jnp.maximum(jnp.maximum(l0, l1), l2)
        e0, e1, e2 = jnp.exp(l0 - m), jnp.exp(l1 - m), jnp.exp(l2 - m)
        den = e0 + e1 + e2
        yb = (e0 / den) * o0 + (e1 / den) * o1 + (e2 / den) * o2
    branches = (_dot(ya_ref[...].astype(BF16), wa_ref[...]),
                _dot(yb.astype(BF16), wb_ref[...]),
                _dot(yc_ref[...].astype(BF16), wc_ref[...]))
    mix = None
    for j, br in enumerate(branches):
        gate = _sigmoid(_dot(u, wg_ref[:, j * d:(j + 1) * d]) + bg_ref[:, j * d:(j + 1) * d])
        mix = gate * br if mix is None else mix + gate * br
    h = _dot(mix.astype(BF16), wo_ref[...])
    y = ALPHA * x + (1.0 + mod_ref[2]) * h
    out_ref[...] = _layer_norm(y, lng_ref[...], lnb_ref[...])


def _mix(x, mod, ya, yb_parts, yc, wg, bg, wa, wb, wc, wo, ln_g, ln_b, tiles_per_seq):
    r, d = x.shape
    tm = min(ROW_TILE, r)
    row = lambda w: pl.BlockSpec((tm, w), lambda i: (i, 0))
    tiles = pl.BlockSpec((B_TILES, tm, LANES), lambda i: (0, i, 0))
    weights = [wg, bg, wa, wb, wc, wo, ln_g, ln_b]
    ins = [x, mod, ya] + list(yb_parts) + [yc] + weights
    in_specs = ([row(d), _mod_spec(mod, tm, tiles_per_seq), row(A_V)]
                + ([row(B_W)] if len(yb_parts) == 1 else [tiles] * len(yb_parts))
                + [row(C_C)] + [_resident(a.shape) for a in weights])
    return pl.pallas_call(
        functools.partial(_mix_kernel, merged=len(yb_parts) == 1),
        grid=(r // tm,),
        in_specs=in_specs,
        out_specs=row(d),
        out_shape=jax.ShapeDtypeStruct((r, d), F32),
        compiler_params=_cparams("parallel"),
        name="mix",
    )(*ins)


RET_SAMPLE_SEQS = 4


def _ret_sample_kernel(q_ref, k_ref, v_ref, dm_ref, qd_ref, kd_ref, gc_ref, st_ref, y_ref, so_ref):
    probs = []
    for b in range(q_ref.shape[0]):
        for h in range(H_A):
            q, k, v = q_ref[b, h], k_ref[b, h], v_ref[b, h].astype(BF16)
            probs.append(dict(b=b, h=h, v=v, att=_dot_nt(q.astype(BF16), k.astype(BF16)) * dm_ref[h],
                              cross=_dot((q * qd_ref[h]).astype(BF16), st_ref[b, h].astype(BF16)),
                              kv=_dot_tn((k * kd_ref[h]).astype(BF16), v)))
    for d in probs:
        b, h = d["b"], d["h"]
        y_ref[b, h] = _dot(d["att"].astype(BF16), d["v"]) + d["cross"]
        so_ref[b, h] = st_ref[b, h] * gc_ref[h] + d["kv"]


def _ret_sample(q, k, v, states, layer, t):
    nb = q.shape[0]
    sb = RET_SAMPLE_SEQS
    dmask, kdec, qdec, gc = _ret_tables(t)
    padt = TOK_PAD - t
    dm = jnp.pad(dmask, ((0, 0), (0, padt), (0, padt)))
    qd = jnp.broadcast_to(jnp.pad(qdec, ((0, 0), (0, padt)))[:, :, None], (H_A, TOK_PAD, DK_A))
    kd = jnp.broadcast_to(jnp.pad(kdec, ((0, 0), (0, padt)))[:, :, None], (H_A, TOK_PAD, DK_A))
    gcb = jnp.broadcast_to(gc[:, None, None], (H_A, DK_A, DV_A))
    tok = lambda w: pl.BlockSpec((sb, H_A, TOK_PAD, w), lambda i: (i, 0, 0, 0))
    return pl.pallas_call(
        _ret_sample_kernel,
        grid=(nb // sb,),
        in_specs=[tok(DK_A), tok(DK_A), tok(DV_A), _resident(dm.shape), _resident(qd.shape),
                  _resident(kd.shape), _resident(gcb.shape),
                  pl.BlockSpec((None, sb, H_A, DK_A, DV_A), lambda i: (layer, i, 0, 0, 0))],
        out_specs=[tok(DV_A), pl.BlockSpec((sb, H_A, DK_A, DV_A), lambda i: (i, 0, 0, 0))],
        out_shape=[jax.ShapeDtypeStruct((nb, H_A, TOK_PAD, DV_A), F32),
                   jax.ShapeDtypeStruct(states.shape[1:], F32)],
        compiler_params=_cparams("parallel"),
        name="ret_sample",
    )(q, k, v, dm, qd, kd, gcb, states)


def _ret_norm_kernel(y_ref, g_ref, gng_ref, gnb_ref, o_ref):
    y = y_ref[...]
    outs = []
    for h in range(H_A):
        yh = y[:, h * DV_A:(h + 1) * DV_A]
        mu = jnp.mean(yh, axis=-1, keepdims=True)
        dd = yh - mu
        var = jnp.mean(dd * dd, axis=-1, keepdims=True)
        outs.append(dd * lax.rsqrt(var + GN_EPS_A))
    yn = jnp.concatenate(outs, axis=1) * gng_ref[...] + gnb_ref[...]
    o_ref[...] = (yn * _silu(g_ref[...].astype(F32))).astype(o_ref.dtype)


def _ret_norm(y, g, gn_g, gn_b):
    r = y.shape[0]
    full = lambda a: pl.BlockSpec(a.shape, lambda: (0,) * a.ndim)
    return pl.pallas_call(
        _ret_norm_kernel,
        in_specs=[full(y), full(g), full(gn_g), full(gn_b)],
        out_specs=pl.BlockSpec((r, A_V), lambda: (0, 0)),
        out_shape=jax.ShapeDtypeStruct((r, A_V), BF16),
        name="ret_norm",
    )(y, g, gn_g, gn_b)


TOK_PAD = 8


def _win_sample_kernel(*refs, t, has_prev):
    ng = len(DIL_GROUPS)
    q_refs, kn_refs, vn_refs = refs[0:ng], refs[ng:2 * ng], refs[2 * ng:3 * ng]
    kv_refs, w_refs = refs[3 * ng:4 * ng], refs[4 * ng:5 * ng]
    base = 5 * ng + (ng if has_prev else 0)
    yb_ref, o_refs = refs[base], refs[base + 1:base + 1 + ng]
    nrow = TOK_PAD
    ncol = lax.broadcasted_iota(jnp.int32, (nrow, nrow), 1)
    ntok = lax.broadcasted_iota(jnp.int32, (nrow, nrow), 0)
    probs = []
    for g, (_, dil) in enumerate(DIL_GROUPS):
        w = w_refs[g].shape[-1]
        lane = lax.broadcasted_iota(jnp.int32, (nrow, w), 1)
        tok = lax.broadcasted_iota(jnp.int32, (nrow, w), 0)
        valid = ((lane >= tok) if dil == 1 else (lane % dil == tok)) | (tok >= t)
        ok_new = (ncol <= ntok) if dil == 1 else (ncol == ntok)
        for h in range(H_B):
            q = q_refs[g][h].astype(BF16)
            probs.append(dict(g=g, h=h, valid=valid, ok_new=ok_new,
                              sc=_dot(q, w_refs[g][0, h].astype(BF16)),
                              sn=_dot_nt(q, kn_refs[g][h].astype(BF16))))
    for d in probs:
        sc = jnp.where(d.pop("valid"), d.pop("sc") * DH_B ** -0.5, -jnp.inf)
        sn = jnp.where(d.pop("ok_new"), d.pop("sn") * DH_B ** -0.5, -jnp.inf)
        m = jnp.maximum(jnp.max(sc, axis=-1, keepdims=True), jnp.max(sn, axis=-1, keepdims=True))
        p = jnp.exp(sc - m)
        pn = jnp.exp(sn - m)
        l = jnp.sum(p, axis=-1, keepdims=True) + jnp.sum(pn, axis=-1, keepdims=True)
        d["lse"] = m + jnp.log(l)
        d["l"] = l
        d["p"] = p.astype(BF16)
        d["pn"] = pn.astype(BF16)
    for d in probs:
        g, h = d["g"], d["h"]
        d["o"] = (_dot_nt(d.pop("p"), w_refs[g][1, h].astype(BF16))
                  + _dot(d.pop("pn"), vn_refs[g][h].astype(BF16))) / d.pop("l")
    for h in range(H_B):
        mine = [d for d in probs if d["h"] == h]
        m = functools.reduce(jnp.maximum, [d["lse"] for d in mine])
        es = [jnp.exp(d["lse"] - m) for d in mine]
        den = functools.reduce(lambda a, b: a + b, es)
        yb_ref[h] = functools.reduce(lambda a, b: a + b, [(e / den) * d["o"] for e, d in zip(es, mine)])
    sel = (lax.broadcasted_iota(jnp.int32, (nrow, LANES), 0) + (LANES - t)
           == lax.broadcasted_iota(jnp.int32, (nrow, LANES), 1)).astype(BF16)
    last = lax.broadcasted_iota(jnp.int32, (DH_B, LANES), 1) >= LANES - t
    for g in range(ng):
        w = w_refs[g].shape[-1]
        new = kv_refs[g][...]
        p1 = new.astype(BF16)
        r1 = new - p1.astype(F32)
        p2 = r1.astype(BF16)
        p3 = (r1 - p2.astype(F32)).astype(BF16)
        cols = _dot_tn(p1, sel) + _dot_tn(p2, sel) + _dot_tn(p3, sel)
        for kv in range(2):
            for h in range(H_B):
                rolled = pltpu.roll(w_refs[g][kv, h], w - t, 1)
                c0 = (kv * H_B + h) * DH_B
                if w > LANES:
                    o_refs[g][kv, h, :, 0:w - LANES] = rolled[:, 0:w - LANES]
                o_refs[g][kv, h, :, w - LANES:w] = jnp.where(last, cols[c0:c0 + DH_B, :],
                                                             rolled[:, w - LANES:w])


def _win_sample(qkv, wins_t, prev, layer, nb, t):
    ng = len(DIL_GROUPS)
    has_prev = prev is not None

    def heads(a):
        a = a.reshape(nb, t, H_B, DH_B).transpose(0, 2, 1, 3)
        return jnp.pad(a, ((0, 0), (0, 0), (0, TOK_PAD - t), (0, 0)))

    qs = [heads(x[:, 0:B_W]) for x in qkv]
    kns = [heads(x[:, B_W:2 * B_W]) for x in qkv]
    vns = [heads(x[:, 2 * B_W:3 * B_W]) for x in qkv]
    kvs = [jnp.pad(x[:, B_W:].reshape(nb, t, 2 * B_W), ((0, 0), (0, TOK_PAD - t), (0, 0))) for x in qkv]
    hspec = pl.BlockSpec((None, H_B, TOK_PAD, DH_B), lambda b: (b, 0, 0, 0))
    kvspec = pl.BlockSpec((None, TOK_PAD, 2 * B_W), lambda b: (b, 0, 0))
    wspec = lambda a: pl.BlockSpec((None, None) + a.shape[2:], lambda b: (layer, b, 0, 0, 0, 0))
    ins = qs + kns + vns + kvs + list(wins_t) + (list(prev) if has_prev else [])
    in_specs = ([hspec] * (3 * ng) + [kvspec] * ng + [wspec(a) for a in wins_t]
                + ([pl.BlockSpec(memory_space=pl.ANY)] * ng if has_prev else []))
    outs = pl.pallas_call(
        functools.partial(_win_sample_kernel, t=t, has_prev=has_prev),
        grid=(nb,),
        in_specs=in_specs,
        out_specs=[hspec] + [wspec(a) for a in wins_t],
        out_shape=[jax.ShapeDtypeStruct((nb, H_B, TOK_PAD, DH_B), F32)]
                  + [jax.ShapeDtypeStruct(a.shape, a.dtype) for a in wins_t],
        input_output_aliases={5 * ng + g: 1 + g for g in range(ng)} if has_prev else {},
        compiler_params=_cparams("parallel"),
        name="win_sample",
    )(*ins)
    yb = outs[0][:, :, :t].transpose(0, 2, 1, 3).reshape(nb * t, B_W)
    return yb, outs[1:]


def _rwkv_prep_kernel(*refs, has_vres):
    it = iter(refs)
    prkv_ref, plora_ref, prev_rkv_ref, prev_lora_ref = (next(it) for _ in range(4))
    vfirst_ref = next(it) if has_vres else None
    bones_ref = next(it)
    prm = {name: next(it) for name in _RWKV_PARAM_NAMES}
    outs = [next(it) for _ in range(9)]
    res = _rwkv_prep(prkv_ref[...], plora_ref[...], prev_rkv_ref[...], prev_lora_ref[...], prm,
                     vfirst_ref[...] if has_vres else None, bones_ref[...])
    r, lw, kmod, v, nkk, beta, g, bonus, v0 = res
    for ref, val in zip(outs, (r, jnp.exp(lw), kmod, v, nkk, beta, g, bonus, v0)):
        ref[...] = val


def _rwkv_prep_call(prkv, plora, prev_rkv, prev_lora, v_first, prm):
    has_vres = v_first is not None
    bones, _ = _rwkv_consts()
    ins = [prkv, plora, prev_rkv, prev_lora] + ([v_first] if has_vres else []) + [bones] + [prm[k] for k in _RWKV_PARAM_NAMES]
    full = lambda a: pl.BlockSpec(a.shape, lambda: (0,) * a.ndim)
    r = prkv.shape[0]
    return pl.pallas_call(
        functools.partial(_rwkv_prep_kernel, has_vres=has_vres),
        in_specs=[full(a) for a in ins],
        out_specs=[pl.BlockSpec((r, C_C), lambda: (0, 0))] * 9,
        out_shape=[jax.ShapeDtypeStruct((r, C_C), F32)] * 9,
        compiler_params=pltpu.CompilerParams(vmem_limit_bytes=VMEM_LIMIT),
        name="rwkv_prep",
    )(*ins)


def _wkv_sample_kernel(r_ref, w_ref, k_ref, v_ref, nkk_ref, beta_ref, st_ref, y_ref, so_ref, *, t):
    for ti in range(t):
        src = st_ref if ti == 0 else so_ref
        nkk_t, w_t, beta_t, k_t, r_t = nkk_ref[ti], w_ref[ti], beta_ref[ti], k_ref[ti], r_ref[ti]

        def step(vi, carry, src=src, ti=ti, nkk_t=nkk_t, w_t=w_t, beta_t=beta_t, k_t=k_t, r_t=r_t):
            s = src[vi]
            sa = jnp.sum(s * nkk_t, axis=0, keepdims=True)
            s_new = s * w_t + sa * beta_t + v_ref[ti, pl.ds(vi, 1), :] * k_t
            so_ref[vi] = s_new
            y_ref[ti, pl.ds(vi, 1), :] = jnp.sum(s_new * r_t, axis=0, keepdims=True)
            return carry

        lax.fori_loop(0, N_C, step, 0, unroll=4)


def _wkv_sample(r, w, k, v, nkk, beta, states, layer, t):
    nb = r.shape[-1]
    per_head = pl.BlockSpec((t, None, N_C, nb), lambda h: (0, h, 0, 0))
    return pl.pallas_call(
        functools.partial(_wkv_sample_kernel, t=t),
        grid=(H_C,),
        in_specs=[per_head] * 6 + [pl.BlockSpec((None, None, N_C, N_C, nb), lambda h: (layer, h, 0, 0, 0))],
        out_specs=[per_head, pl.BlockSpec((None, N_C, N_C, nb), lambda h: (h, 0, 0, 0))],
        out_shape=[jax.ShapeDtypeStruct((t, H_C, N_C, nb), F32),
                   jax.ShapeDtypeStruct(states.shape[1:], F32)],
        compiler_params=_cparams("parallel"),
        name="wkv_sample",
    )(r, w, k, v, nkk, beta, states)


def _rwkv_post_kernel(y_ref, bonus_ref, g_ref, bones_ref, lnxg_ref, lnxb_ref, o_ref):
    prm = {"lnx_g": lnxg_ref, "lnx_b": lnxb_ref}
    o_ref[...] = _rwkv_post(y_ref[...], bonus_ref[...], g_ref[...], prm, bones_ref[...]).astype(o_ref.dtype)


def _rwkv_post_call(y, bonus, g, prm):
    bones, _ = _rwkv_consts()
    ins = [y, bonus, g, bones, prm["lnx_g"], prm["lnx_b"]]
    full = lambda a: pl.BlockSpec(a.shape, lambda: (0,) * a.ndim)
    r = y.shape[0]
    return pl.pallas_call(
        _rwkv_post_kernel,
        in_specs=[full(a) for a in ins],
        out_specs=pl.BlockSpec((r, C_C), lambda: (0, 0)),
        out_shape=jax.ShapeDtypeStruct((r, C_C), BF16),
        name="rwkv_post",
    )(*ins)


def _rotary_tables(pos):
    half = DK_A // 2
    inv = 1.0 / (ROPE_BASE_A ** jnp.linspace(0.0, 1.0, half, dtype=F32))
    ang = pos.astype(F32)[:, None] * inv[None, :]
    cos, sin = jnp.cos(ang), jnp.sin(ang)
    return jnp.tile(cos, (1, 4)), jnp.tile(jnp.concatenate([-sin, sin], axis=1), (1, 2))


def _rwkv_params(l, w):
    row = lambda a: a.reshape(1, -1)
    pad_rows = lambda a, r0: jnp.zeros((LORA_COLS, a.shape[1]), BF16).at[r0:r0 + a.shape[0]].set(a.astype(BF16))
    prm = {
        "mu_rkv": row(w["rw_mu"][l, :3 * C_C]), "mu_lora": row(w["rw_mu"][l, 3 * C_C:]),
        "w0": row(w["rw_w0"][l]), "w2": pad_rows(w["rw_w2"][l], 0),
        "a0": row(w["rw_a0"][l]), "a2": pad_rows(w["rw_a2"][l], LORA_W),
        "g2": pad_rows(w["rw_g2"][l], LORA_W + LORA_A),
        "k_k": row(w["rw_kk"][l]), "k_a": row(w["rw_ka"][l]), "r_k": row(w["rw_rk"][l]),
        "lnx_g": row(w["rw_lnx_g"][l]), "lnx_b": row(w["rw_lnx_b"][l]),
    }
    lv = max(l - 1, 0)
    prm["v0"] = row(w["rw_v0"][lv])
    prm["vr1"] = w["rw_vr1"][lv].astype(BF16)
    prm["vr2"] = w["rw_vr2"][lv].astype(BF16)
    return prm


def _layer_weights(l, w):
    d = w["w_in"].shape[1]
    return dict(
        ffn1_up=w["ffn1_up"][l].astype(BF16), ffn1_dn=w["ffn1_down"][l].astype(BF16),
        ffn2_up=w["ffn2_up"][l].astype(BF16), ffn2_dn=w["ffn2_down"][l].astype(BF16),
        w_mix=w["w_in"][l, :, :G_OFF].astype(BF16), w_gate=w["w_in"][l, :, G_OFF:].astype(BF16),
        b_gate=w["b_gate"][l].reshape(1, -1),
        wa=w["w_proj_a"][l].astype(BF16), wb=w["w_proj_b"][l].astype(BF16),
        wc=w["w_proj_c"][l].astype(BF16), wo=w["w_out"][l].astype(BF16),
        ln_g=[w["ln_g"][l, j].reshape(1, d) for j in range(3)],
        ln_b=[w["ln_b"][l, j].reshape(1, d) for j in range(3)],
        gn_g=w["ret_gn_g"][l].reshape(1, -1), gn_b=w["ret_gn_b"][l].reshape(1, -1),
        rwkv=_rwkv_params(l, w),
    )


def _prompt_layer(x, m, lw, tabs, v_first, n, s):
    tps = s // min(ROW_TILE, s)
    x = _ffn(x, m[0:3], lw["ffn1_up"], lw["ffn1_dn"], lw["ln_g"][0], lw["ln_b"][0], tps)
    qa, ka, va, ga, b0, b1, b2, prkv, plora = _proj(x, m[3:6], lw["w_mix"], tabs[0], tabs[1], tps)
    ya, r_fin = _ret_prompt(qa, ka, va, ga, lw["gn_g"], lw["gn_b"], n, s)
    ob, lb, wins = [], [], []
    for qkv, (win, dil) in zip((b0, b1, b2), DIL_GROUPS):
        o, lse = _dil_prompt(qkv, n, s, dil)
        ob.append(o)
        lb.append(lse)
        wp = min(win, s)
        kv = qkv.reshape(3, B_TILES, n, s, H_B // B_TILES, DH_B)[1:3, :, :, s - wp:]
        wins.append(kv.transpose(2, 3, 0, 1, 4, 5).reshape(n, wp, 2, H_B, DH_B))
    yc, v_out, s_fin = _rwkv_prompt(prkv, plora, v_first, lw["rwkv"], n, s)
    if v_first is None:
        v_first = v_out
    sh_new = jnp.concatenate([prkv.reshape(n, s, -1)[:, -1], plora.reshape(n, s, -1)[:, -1]], axis=-1)
    x = _mix(x, m[3:6], ya, ob + lb, yc, lw["w_gate"], lw["b_gate"], lw["wa"], lw["wb"], lw["wc"], lw["wo"],
             lw["ln_g"][1], lw["ln_b"][1], tps)
    x = _ffn(x, m[6:9], lw["ffn2_up"], lw["ffn2_dn"], lw["ln_g"][2], lw["ln_b"][2], tps)
    return x, v_first, (r_fin, s_fin, sh_new, wins)


def _to_lanes(a, nb, t, heads):
    return a.reshape(nb, t, heads, -1).transpose(1, 2, 3, 0)


def _from_lanes(a):
    t, heads, w, nb = a.shape
    return a.transpose(3, 0, 1, 2).reshape(nb * t, heads * w)


def _token_heads(a, nb, t, heads):
    a = a.astype(F32).reshape(nb, t, heads, -1).transpose(0, 2, 1, 3)
    return jnp.pad(a, ((0, 0), (0, 0), (0, TOK_PAD - t), (0, 0)))


def _sample_layer(x, m, lw, tabs, v_first, l, past, new_wins, nb, t):
    state_ret, state_wkv_t, state_shift, wins_t = past
    x = _ffn(x, m[0:3], lw["ffn1_up"], lw["ffn1_dn"], lw["ln_g"][0], lw["ln_b"][0], 1)
    qa, ka, va, ga, b0, b1, b2, prkv, plora = _proj(x, m[3:6], lw["w_mix"], tabs[0], tabs[1], 1)
    b0, b1, b2 = (b.transpose(1, 0, 2).reshape(nb * t, 3 * B_W) for b in (b0, b1, b2))
    y_h, r_new = _ret_sample(_token_heads(qa, nb, t, H_A), _token_heads(ka, nb, t, H_A),
                             _token_heads(va, nb, t, H_A), state_ret, l, t)
    ya = _ret_norm(y_h[:, :, :t].transpose(0, 2, 1, 3).reshape(nb * t, A_V), ga, lw["gn_g"], lw["gn_b"])
    yb, new_wins = _win_sample((b0, b1, b2), wins_t, new_wins, l, nb, t)
    prev_rkv = jnp.concatenate([state_shift[l][:, None, :3 * C_C], prkv.reshape(nb, t, -1)[:, :-1]], axis=1)
    prev_lora = jnp.concatenate([state_shift[l][:, None, 3 * C_C:], plora.reshape(nb, t, -1)[:, :-1]], axis=1)
    r, dec, kmod, v, nkk, beta, g, bonus, v0 = _rwkv_prep_call(
        prkv, plora, prev_rkv.reshape(nb * t, -1), prev_lora.reshape(nb * t, -1), v_first, lw["rwkv"])
    if v_first is None:
        v_first = v0
    y_l, s_new_l = _wkv_sample(*[_to_lanes(a, nb, t, H_C) for a in (r, dec, kmod, v, nkk, beta)],
                               state_wkv_t, l, t)
    s_new = s_new_l.transpose(3, 0, 1, 2)
    yc = _rwkv_post_call(_from_lanes(y_l), bonus, g, lw["rwkv"])
    sh_new = jnp.concatenate([prkv.reshape(nb, t, -1)[:, -1], plora.reshape(nb, t, -1)[:, -1]], axis=-1)
    x = _mix(x, m[3:6], ya, [yb], yc, lw["w_gate"], lw["b_gate"], lw["wa"], lw["wb"], lw["wc"], lw["wo"],
             lw["ln_g"][1], lw["ln_b"][1], 1)
    x = _ffn(x, m[6:9], lw["ffn2_up"], lw["ffn2_dn"], lw["ln_g"][2], lw["ln_b"][2], 1)
    return x, v_first, (r_new, s_new, sh_new), new_wins


def kernel(x_prompt, x_sample, state_ret, state_wkv, state_shift, cache_win1, cache_win2, cache_win3, c_prompt, c_sample, w_ada, b_ada, ln_g, ln_b, ffn1_up, ffn1_down, ffn2_up, ffn2_down, w_in, b_gate, ret_gn_g, ret_gn_b, rw_mu, rw_w0, rw_w2, rw_a0, rw_a2, rw_g2, rw_kk, rw_ka, rw_rk, rw_lnx_g, rw_lnx_b, rw_v0, rw_vr1, rw_vr2, w_proj_a, w_proj_b, w_proj_c, w_out):
    w = dict(w_ada=w_ada, b_ada=b_ada, ln_g=ln_g, ln_b=ln_b, ffn1_up=ffn1_up, ffn1_down=ffn1_down,
             ffn2_up=ffn2_up, ffn2_down=ffn2_down, w_in=w_in, b_gate=b_gate, ret_gn_g=ret_gn_g,
             ret_gn_b=ret_gn_b, rw_mu=rw_mu, rw_w0=rw_w0, rw_w2=rw_w2, rw_a0=rw_a0, rw_a2=rw_a2,
             rw_g2=rw_g2, rw_kk=rw_kk, rw_ka=rw_ka, rw_rk=rw_rk, rw_lnx_g=rw_lnx_g, rw_lnx_b=rw_lnx_b,
             rw_v0=rw_v0, rw_vr1=rw_vr1, rw_vr2=rw_vr2, w_proj_a=w_proj_a, w_proj_b=w_proj_b,
             w_proj_c=w_proj_c, w_out=w_out)
    n, s, d = x_prompt.shape
    nb, t, _ = x_sample.shape
    depth = w_ada.shape[0]
    past_len = cache_win3.shape[2]

    m_all = _ada(jnp.concatenate([c_prompt, c_sample], axis=0), w_ada, b_ada).reshape(depth, n + nb, 9, d)
    tabs_p = _rotary_tables(jnp.arange(s))
    tabs_s = tuple(jnp.tile(tb, (nb, 1)) for tb in _rotary_tables(past_len + jnp.arange(t)))
    caches = (cache_win1, cache_win2, cache_win3)
    for cache, (win, _) in zip(caches, DIL_GROUPS):
        assert cache.shape[2] == win, "each window buffer must hold its whole window"
    wins_t = tuple(c.transpose(0, 1, 3, 4, 5, 2) for c in caches)
    past = (state_ret, state_wkv.transpose(0, 2, 3, 4, 1), state_shift, wins_t)

    xp = x_prompt.reshape(n * s, d)
    xs = x_sample.reshape(nb * t, d)
    vf_p = vf_s = None
    new_wins = None
    outs_p, outs_s = [], []
    for l in range(depth):
        lw = _layer_weights(l, w)
        m_p = m_all[l, :n].transpose(1, 0, 2)[:, :, None, :]
        m_s = jnp.repeat(m_all[l, n:], t, axis=0).transpose(1, 0, 2)
        xp, vf_p, st_p = _prompt_layer(xp, m_p, lw, tabs_p, vf_p, n, s)
        xs, vf_s, st_s, new_wins = _sample_layer(xs, m_s, lw, tabs_s, vf_s, l, past, new_wins, nb, t)
        outs_p.append(st_p)
        outs_s.append(st_s)

    stack = lambda outs, j: jnp.stack([o[j] for o in outs])
    p_wins = [jnp.stack([o[3][gi] for o in outs_p]) for gi in range(len(DIL_GROUPS))]
    s_wins = [a.transpose(0, 1, 5, 2, 3, 4) for a in new_wins]
    return (xp.reshape(n, s, d), xs.reshape(nb, t, d), stack(outs_p, 0), stack(outs_p, 1), stack(outs_p, 2),
            *p_wins, stack(outs_s, 0), stack(outs_s, 1), stack(outs_s, 2), *s_wins)
```

```python
import functools
import math

import numpy as np
import jax
import jax.numpy as jnp
from jax import lax
from jax.experimental import pallas as pl
from jax.experimental.pallas import tpu as pltpu

F32 = jnp.float32
BF16 = jnp.bfloat16

H_A, DK_A, DV_A = 4, 64, 128
CHUNK_A = 128
ROPE_BASE_A = 10000.0
GN_EPS_A = 1e-5
DIL_GROUPS = ((128, 1), (512, 4), (2048, 16))
H_B, DH_B = 4, 64
BAND_B = 128
H_C, N_C = 8, 64
C_C = H_C * N_C
LORA_W, LORA_A, LORA_G = 64, 64, 160
GN_EPS_C = 64e-5
LN_EPS = 1e-5
DEPTH = 4
ALPHA = (2 * DEPTH) ** 0.25
A_QK = H_A * DK_A
A_V = H_A * DV_A
A_COLS = 2 * A_QK + 2 * A_V
B_W = H_B * DH_B
B_COLS = len(DIL_GROUPS) * 3 * B_W
C_OFF = A_COLS + B_COLS
LORA_COLS = LORA_W + LORA_A + LORA_G
C_SHIFT_W = 3 * C_C + LORA_COLS
G_OFF = C_OFF + C_SHIFT_W

LANES = 128
WKV_CHUNK = 64
ROW_TILE = 512
VMEM_LIMIT = 56 * 1024 * 1024


def _cparams(*sem):
    return pltpu.CompilerParams(dimension_semantics=sem, vmem_limit_bytes=VMEM_LIMIT)


def _dot(a, b):
    return jnp.dot(a, b, preferred_element_type=F32)


def _dot_nt(a, b):
    return lax.dot_general(a, b, (((1,), (1,)), ((), ())), preferred_element_type=F32)


def _dot_tn(a, b):
    return lax.dot_general(a, b, (((0,), (0,)), ((), ())), preferred_element_type=F32)


def _sigmoid(x):
    return 1.0 / (1.0 + jnp.exp(-x))


def _silu(x):
    return x * _sigmoid(x)


def _layer_norm(y, g, b):
    mu = jnp.mean(y, axis=-1, keepdims=True)
    d = y - mu
    var = jnp.mean(d * d, axis=-1, keepdims=True)
    return d * lax.rsqrt(var + LN_EPS) * g + b


def _resident(shape):
    nd = len(shape)
    return pl.BlockSpec(shape, lambda *_: (0,) * nd, pipeline_mode=pl.Buffered(1))


def _mod_spec(mod, tm, tiles_per_seq):
    if mod.ndim == 4:
        k, _, _, d = mod.shape
        return pl.BlockSpec((k, None, 1, d), lambda i: (0, i // tiles_per_seq, 0, 0))
    k, _, d = mod.shape
    return pl.BlockSpec((k, tm, d), lambda i: (0, i, 0))


def _ada_kernel(c_ref, w_ref, b_ref, o_ref):
    c = c_ref[...]
    o_ref[...] = _dot(_silu(c).astype(BF16), w_ref[...].astype(BF16)) + b_ref[...]


def _ada(c_all, w_ada, b_ada):
    nl, d, nd = w_ada.shape
    nc = c_all.shape[0]
    tn = 1024
    return pl.pallas_call(
        _ada_kernel,
        grid=(nl, nd // tn),
        in_specs=[
            pl.BlockSpec((nc, d), lambda l, j: (0, 0)),
            pl.BlockSpec((None, d, tn), lambda l, j: (l, 0, j)),
            pl.BlockSpec((None, 1, tn), lambda l, j: (l, 0, j)),
        ],
        out_specs=pl.BlockSpec((None, nc, tn), lambda l, j: (l, 0, j)),
        out_shape=jax.ShapeDtypeStruct((nl, nc, nd), F32),
        compiler_params=_cparams("parallel", "parallel"),
        name="ada",
    )(c_all, w_ada, b_ada.reshape(nl, 1, nd))


def _ffn_kernel(x_ref, mod_ref, wup_ref, wdn_ref, lng_ref, lnb_ref, o_ref, acc_ref, *, fc):
    x = x_ref[...]
    u = (x * (1.0 + mod_ref[1]) + mod_ref[0]).astype(BF16)
    dff = wdn_ref.shape[0]
    for c in range(dff // fc):
        a = _dot(u, wup_ref[:, c * fc:(c + 1) * fc])
        b = _dot(u, wup_ref[:, dff + c * fc:dff + (c + 1) * fc])
        g = (_silu(a) * b).astype(BF16)
        d = _dot(g, wdn_ref[c * fc:(c + 1) * fc, :])
        if c == 0:
            acc_ref[...] = d
        else:
            acc_ref[...] += d
    y = ALPHA * x + 0.5 * (1.0 + mod_ref[2]) * acc_ref[...]
    o_ref[...] = _layer_norm(y, lng_ref[...], lnb_ref[...])


def _ffn(x, mod, w_up, w_dn, ln_g, ln_b, tiles_per_seq):
    r, d = x.shape
    dff = w_dn.shape[0]
    tm = min(ROW_TILE, r)
    return pl.pallas_call(
        functools.partial(_ffn_kernel, fc=256),
        grid=(r // tm,),
        in_specs=[
            pl.BlockSpec((tm, d), lambda i: (i, 0)),
            _mod_spec(mod, tm, tiles_per_seq),
            _resident((d, 2 * dff)),
            _resident((dff, d)),
            _resident((1, d)),
            _resident((1, d)),
        ],
        out_specs=pl.BlockSpec((tm, d), lambda i: (i, 0)),
        out_shape=jax.ShapeDtypeStruct((r, d), F32),
        scratch_shapes=[pltpu.VMEM((tm, d), F32)],
        compiler_params=_cparams("parallel"),
        name="ffn",
    )(x, mod, w_up, w_dn, ln_g, ln_b)


def _swap_halves(x):
    lane = lax.broadcasted_iota(jnp.int32, x.shape, 1)
    fwd = pltpu.roll(x, LANES - 32, 1)
    bwd = pltpu.roll(x, 32, 1)
    return jnp.where((lane % 64) < 32, fwd, bwd)


def _proj_kernel(x_ref, mod_ref, w_ref, cos_ref, sin_ref,
                 qa_ref, ka_ref, va_ref, ga_ref, b0_ref, b1_ref, b2_ref, prkv_ref, plora_ref):
    x = x_ref[...]
    u = (x * (1.0 + mod_ref[1]) + mod_ref[0]).astype(BF16)
    cos = cos_ref[...]
    sin = sin_ref[...]

    def cols(c0, c1):
        return _dot(u, w_ref[:, c0:c1])

    for ref, base, scale in ((qa_ref, 0, 1.0), (ka_ref, A_QK, DK_A ** -0.5)):
        z = cols(base, base + A_QK)
        for t in range(A_QK // LANES):
            zt = z[:, t * LANES:(t + 1) * LANES]
            rot = zt * cos + _swap_halves(zt) * sin
            ref[:, t * LANES:(t + 1) * LANES] = (rot * scale).astype(ref.dtype)
    va_ref[...] = cols(2 * A_QK, 2 * A_QK + A_V).astype(va_ref.dtype)
    ga_ref[...] = cols(2 * A_QK + A_V, A_COLS).astype(ga_ref.dtype)
    for gi, ref in enumerate((b0_ref, b1_ref, b2_ref)):
        base = A_COLS + gi * 3 * B_W
        z = cols(base, base + 3 * B_W)
        for c in range(3 * B_W // LANES):
            ref[c] = z[:, c * LANES:(c + 1) * LANES]
    prkv_ref[...] = cols(C_OFF, C_OFF + 3 * C_C)
    plora_ref[...] = cols(C_OFF + 3 * C_C, G_OFF)


def _proj(x, mod, w_mix, cos_t, sin_t, tiles_per_seq):
    r, d = x.shape
    tm = min(ROW_TILE, r)
    row = lambda w: pl.BlockSpec((tm, w), lambda i: (i, 0))
    tab = pl.BlockSpec((tm, LANES), lambda i: (i % tiles_per_seq, 0))
    rows_out = lambda w, dt: (row(w), jax.ShapeDtypeStruct((r, w), dt))
    tiles_out = (pl.BlockSpec((3 * B_TILES, tm, LANES), lambda i: (0, i, 0)),
                 jax.ShapeDtypeStruct((3 * B_TILES, r, LANES), F32))
    outs = [rows_out(A_QK, BF16), rows_out(A_QK, BF16), rows_out(A_V, BF16), rows_out(A_V, BF16),
            tiles_out, tiles_out, tiles_out, rows_out(3 * C_C, F32), rows_out(LORA_COLS, F32)]
    return pl.pallas_call(
        _proj_kernel,
        grid=(r // tm,),
        in_specs=[row(d), _mod_spec(mod, tm, tiles_per_seq), _resident((d, G_OFF)), tab, tab],
        out_specs=[spec for spec, _ in outs],
        out_shape=[shape for _, shape in outs],
        compiler_params=_cparams("parallel"),
        name="proj",
    )(x, mod, w_mix, cos_t, sin_t)


def _ret_prompt_kernel(q_ref, k_ref, v_ref, g_ref, dmask_ref, qdec_ref, kdec_ref, gc_ref, bd_ref,
                       gng_ref, gnb_ref, y_ref, rfin_ref, st_ref, *, nchunk):
    i = pl.program_id(1)

    @pl.when(i == 0)
    def _():
        st_ref[...] = jnp.zeros_like(st_ref)

    c = CHUNK_A
    lane_head = lax.broadcasted_iota(jnp.int32, (1, A_QK), 1) // DK_A
    head_sel = [(lane_head == h).astype(F32).astype(BF16) for h in range(H_A)]
    for ci in range(nchunk):
        rows = slice(ci * c, (ci + 1) * c)
        q = q_ref[rows, :]
        k = k_ref[rows, :]
        v = v_ref[rows, :]
        st = st_ref[...]
        ys = []
        for h in range(H_A):
            att = _dot_nt(q * head_sel[h], k) * dmask_ref[h]
            ys.append(_dot(att.astype(BF16), v[:, h * DV_A:(h + 1) * DV_A]))
        y = jnp.concatenate(ys, axis=1)
        y = y + _dot((q.astype(F32) * qdec_ref[...]).astype(BF16), st.astype(BF16))
        kv = _dot_tn((k.astype(F32) * kdec_ref[...]).astype(BF16), v)
        st_ref[...] = st * gc_ref[...] + kv * bd_ref[...]
        outs = []
        for h in range(H_A):
            yh = y[:, h * DV_A:(h + 1) * DV_A]
            mu = jnp.mean(yh, axis=-1, keepdims=True)
            dd = yh - mu
            var = jnp.mean(dd * dd, axis=-1, keepdims=True)
            outs.append(dd * lax.rsqrt(var + GN_EPS_A))
        yn = jnp.concatenate(outs, axis=1) * gng_ref[...] + gnb_ref[...]
        y_ref[rows, :] = (yn * _silu(g_ref[rows, :].astype(F32))).astype(y_ref.dtype)

    @pl.when(i == pl.num_programs(1) - 1)
    def _():
        rfin_ref[...] = st_ref[...]


def _ret_tables(c):
    log_g = jnp.log(1.0 - 2.0 ** (-5.0 - jnp.arange(H_A, dtype=F32)))
    idx = jnp.arange(c, dtype=F32)
    diff = idx[:, None] - idx[None, :]
    dmask = jnp.where(diff >= 0, jnp.exp(jnp.maximum(diff, 0.0)[None] * log_g[:, None, None]), 0.0)
    kdec = jnp.exp((c - 1 - idx)[None, :] * log_g[:, None])
    qdec = jnp.exp((idx + 1)[None, :] * log_g[:, None])
    gc = jnp.exp(c * log_g)
    return dmask, kdec, qdec, gc


def _ret_prompt(qa, ka, va, ga, gn_g, gn_b, n, s):
    tr = min(ROW_TILE, s)
    nt = s // tr
    dmask, kdec, qdec, gc = _ret_tables(CHUNK_A)
    qdec_t = jnp.repeat(qdec.T, DK_A, axis=1)
    kdec_t = jnp.repeat(kdec.T, DK_A, axis=1)
    gc_t = jnp.broadcast_to(jnp.repeat(gc, DK_A)[:, None], (A_QK, A_V))
    bd = (jnp.arange(A_QK)[:, None] // DK_A == jnp.arange(A_V)[None, :] // DV_A).astype(F32)
    row = lambda w: pl.BlockSpec((tr, w), lambda b, i: (b * nt + i, 0))
    y, rfin = pl.pallas_call(
        functools.partial(_ret_prompt_kernel, nchunk=tr // CHUNK_A),
        grid=(n, nt),
        in_specs=[row(A_QK), row(A_QK), row(A_V), row(A_V),
                  _resident((H_A, CHUNK_A, CHUNK_A)), _resident((CHUNK_A, A_QK)),
                  _resident((CHUNK_A, A_QK)), _resident((A_QK, A_V)), _resident((A_QK, A_V)),
                  _resident((1, A_V)), _resident((1, A_V))],
        out_specs=[row(A_V), pl.BlockSpec((None, A_QK, A_V), lambda b, i: (b, 0, 0))],
        out_shape=[jax.ShapeDtypeStruct((n * s, A_V), BF16),
                   jax.ShapeDtypeStruct((n, A_QK, A_V), F32)],
        scratch_shapes=[pltpu.VMEM((A_QK, A_V), F32)],
        compiler_params=_cparams("parallel", "arbitrary"),
        name="ret_prompt",
    )(qa, ka, va, ga, dmask, qdec_t, kdec_t, gc_t, bd, gn_g, gn_b)
    r4 = rfin.reshape(n, H_A, DK_A, H_A, DV_A)
    r_fin = jnp.stack([r4[:, h, :, h, :] for h in range(H_A)], axis=1)
    return y, r_fin


B_TILES = B_W // LANES


def _lane_tiles(ref, rows, first):
    return jnp.concatenate([ref[first + c, rows, :] for c in range(B_TILES)], axis=1)


def _dil_prompt_kernel(cur_ref, prev_ref, o_ref, lse_ref, *, dil, nband):
    i = pl.program_id(1)
    bb = BAND_B
    qi = lax.broadcasted_iota(jnp.int32, (bb, 2 * bb), 0)
    ki = lax.broadcasted_iota(jnp.int32, (bb, 2 * bb), 1)
    band = (ki >= qi) & (ki <= qi + bb)
    lane_head = lax.broadcasted_iota(jnp.int32, (1, B_W), 1) // DH_B
    head_sel = [(lane_head == h).astype(F32).astype(BF16) for h in range(H_B)]
    first_key = jnp.where(i > 0, 0, bb)

    def rows_of(r, j):
        start = r + dil * bb * j
        return pl.ds(start, bb) if dil == 1 else pl.ds(start, bb, stride=dil)

    def residue(r):
        for j in range(nband):
            rows = rows_of(r, j)
            src, prows = (prev_ref, rows_of(r, 0)) if j == 0 else (cur_ref, rows_of(r, j - 1))
            valid = band & (ki >= first_key) if j == 0 else band
            q = _lane_tiles(cur_ref, rows, 0).astype(BF16)
            k = jnp.concatenate([_lane_tiles(src, prows, B_TILES), _lane_tiles(cur_ref, rows, B_TILES)],
                                axis=0).astype(BF16)
            v = jnp.concatenate([_lane_tiles(src, prows, 2 * B_TILES),
                                 _lane_tiles(cur_ref, rows, 2 * B_TILES)], axis=0).astype(BF16)
            scs = [_dot_nt(q * head_sel[h], k) for h in range(H_B)]
            o = jnp.zeros((bb, B_W), F32)
            lse_full = jnp.zeros((bb, B_W), F32)
            for h in range(H_B):
                sc = jnp.where(valid, scs[h] * DH_B ** -0.5, -jnp.inf)
                m = jnp.max(sc, axis=-1, keepdims=True)
                p = jnp.exp(sc - m)
                l = jnp.sum(p, axis=-1, keepdims=True)
                o = o + _dot((p / l).astype(BF16), v * head_sel[h])
                lse_full = lse_full + jnp.where(lane_head == h, m + jnp.log(l), 0.0)
            for c in range(B_TILES):
                o_ref[c, rows, :] = o[:, c * LANES:(c + 1) * LANES]
                lse_ref[c, rows, :] = lse_full[:, c * LANES:(c + 1) * LANES]

    if dil == 1:
        residue(0)
    else:
        def body(r, carry):
            residue(r)
            return carry

        lax.fori_loop(0, dil, body, 0, unroll=min(dil, 4))


def _dil_prompt(qkv, n, s, dil):
    prev_rows = BAND_B * dil
    nband = max(1, ROW_TILE // prev_rows)
    st = prev_rows * nband
    nst = s // st
    cur = pl.BlockSpec((3 * B_TILES, st, LANES), lambda b, i: (0, b * nst + i, 0))
    prev = pl.BlockSpec((3 * B_TILES, prev_rows, LANES),
                        lambda b, i: (0, jnp.maximum((b * nst + i) * nband - 1, 0), 0))
    out = pl.BlockSpec((B_TILES, st, LANES), lambda b, i: (0, b * nst + i, 0))
    return pl.pallas_call(
        functools.partial(_dil_prompt_kernel, dil=dil, nband=nband),
        grid=(n, nst),
        in_specs=[cur, prev],
        out_specs=[out, out],
        out_shape=[jax.ShapeDtypeStruct((B_TILES, n * s, LANES), F32)] * 2,
        compiler_params=_cparams("parallel", "arbitrary"),
        name="dil_prompt",
    )(qkv, qkv)


def _head_sum(x, bones):
    hi = x.astype(BF16)
    lo = (x - hi.astype(F32)).astype(BF16)
    return _dot(hi, bones) + _dot(lo, bones)


def _softplus(x):
    return jnp.maximum(x, 0.0) + jnp.log(1.0 + jnp.exp(-jnp.abs(x)))


def _rwkv_prep(prkv, plora, prev_rkv, prev_lora, prm, v_first, bones):
    xs = prkv + (prev_rkv - prkv) * prm["mu_rkv"][...]
    xl = plora + (prev_lora - plora) * prm["mu_lora"][...]
    r = xs[:, 0:C_C]
    k = xs[:, C_C:2 * C_C]
    v0 = xs[:, 2 * C_C:3 * C_C]
    wlog = -_softplus(-(prm["w0"][...] + _dot(jnp.tanh(xl).astype(BF16), prm["w2"][...]))) - 0.5
    lw = -jnp.exp(wlog)
    a = _sigmoid(prm["a0"][...] + _dot(xl.astype(BF16), prm["a2"][...]))
    g = _dot(_sigmoid(xl).astype(BF16), prm["g2"][...])
    if v_first is None:
        v = v0
    else:
        mixv = _dot(_dot(v0.astype(BF16), prm["vr1"][...]).astype(BF16), prm["vr2"][...])
        v = v0 + (v_first - v0) * _sigmoid(prm["v0"][...] + mixv)
    kk = k * prm["k_k"][...]
    kk = kk / jnp.maximum(jnp.sqrt(_head_sum(kk * kk, bones)), 1e-12)
    kmod = k * (1.0 + (a - 1.0) * prm["k_a"][...])
    bonus = _head_sum(r * kmod * prm["r_k"][...], bones) * v
    return r, lw, kmod, v, -kk, kk * a, g, bonus, v0


def _rwkv_post(y, bonus, g, prm, bones):
    mu = _head_sum(y, bones) * (1.0 / N_C)
    d = y - mu
    var = _head_sum(d * d, bones) * (1.0 / N_C)
    yn = d * lax.rsqrt(var + GN_EPS_C) * prm["lnx_g"][...] + prm["lnx_b"][...]
    return (yn + bonus) * g


_RWKV_PARAM_NAMES = ("mu_rkv", "mu_lora", "w0", "w2", "a0", "a2", "g2", "k_k", "k_a", "r_k",
                     "lnx_g", "lnx_b", "v0", "vr1", "vr2")


def _shift_rows(x, first_row):
    rolled = pltpu.roll(x, 1, 0)
    row = lax.broadcasted_iota(jnp.int32, x.shape, 0)
    return jnp.where(row == 0, first_row, rolled)


NPAIR = C_C // LANES
WKV_SETUP_CHUNKS = 4


def _wkv_masks():
    c, hw = WKV_CHUNK, N_C
    ti = lax.broadcasted_iota(jnp.int32, (c, 2 * hw), 0)
    si = lax.broadcasted_iota(jnp.int32, (c, 2 * hw), 1) % hw
    bdm = (lax.broadcasted_iota(jnp.int32, (2 * c, 2 * hw), 0) // c
           == lax.broadcasted_iota(jnp.int32, (2 * c, 2 * hw), 1) // hw)
    return si < ti, si <= ti, (si == ti).astype(F32), bdm


def _block_diag(x, bdm):
    return jnp.where(bdm, jnp.concatenate([x, x], axis=0), 0.0).astype(BF16)


def _wkv_setup(chunks, ltri, r_ref, lw_ref, kmod_ref, v_ref, nkk_ref, beta_ref,
               lhs_s, uhat_s, arbk_s, bkd_s, ecl_s):
    c, hw = WKV_CHUNK, N_C
    strict, incl, ipair, bdm = _wkv_masks()
    bdm4 = jnp.concatenate([bdm, bdm], axis=0)
    probs = []
    for ci in chunks:
        rows = pl.ds(pl.multiple_of(ci * c, c), c)
        lw_all = lw_ref[rows, :]
        h1 = lw_all.astype(BF16)
        r1 = lw_all - h1.astype(F32)
        h2 = r1.astype(BF16)
        h3 = (r1 - h2.astype(F32)).astype(BF16)
        cs_all = _dot(ltri, h1) + _dot(ltri, h2) + _dot(ltri, h3)
        for p in range(NPAIR):
            sl = slice(p * LANES, (p + 1) * LANES)
            q = ci * NPAIR + p
            cs = cs_all[:, sl]
            cl = cs[c - 1:c, :]
            inv_p = jnp.exp(-cs)
            dec_end = jnp.exp(cl - cs)
            beta = beta_ref[rows, sl]
            kmod = kmod_ref[rows, sl]
            a_t = nkk_ref[rows, sl] * jnp.exp(cs - lw_all[:, sl])
            r_t = (r_ref[rows, sl] * jnp.exp(cs)).astype(BF16)
            b_t = beta * inv_p
            k_t = kmod * inv_p
            lhs_s[q, c:2 * c, :] = r_t
            bkd_s[q] = jnp.concatenate([beta * dec_end, kmod * dec_end], axis=0).astype(BF16)
            ecl_s[q] = jnp.exp(cl)
            x1 = jnp.concatenate([a_t.astype(BF16), r_t], axis=0)
            bk = jnp.where(bdm4, jnp.concatenate([b_t, b_t, k_t, k_t], axis=0), 0.0).astype(BF16)
            probs.append(dict(q=q, a_t=a_t, x1=x1, bk=bk, v=v_ref[rows, sl]))
    for d in probs:
        d["gm"] = _dot_nt(d.pop("x1"), d.pop("bk"))
    for d in probs:
        gm = d.pop("gm")
        d["a_ab"] = jnp.where(strict, gm[0:c, 0:2 * hw], 0.0)
        d["a_ak"] = jnp.where(strict, gm[0:c, 2 * hw:4 * hw], 0.0).astype(BF16)
        arbk_s[d["q"]] = jnp.concatenate([jnp.where(incl, gm[c:2 * c, 0:2 * hw], 0.0),
                                          jnp.where(incl, gm[c:2 * c, 2 * hw:4 * hw], 0.0)],
                                         axis=1).astype(BF16)
    for d in probs:
        a_ab = d.pop("a_ab")
        d["npow"] = _dot(a_ab.astype(BF16), _block_diag(a_ab, bdm))
        d["tm"] = ipair + a_ab
        d["w1"] = _dot(d.pop("a_ak"), _block_diag(d.pop("v"), bdm))
    for j in range(1, int(math.log2(c))):
        for d in probs:
            stacked = jnp.concatenate([d["npow"], d["tm"]], axis=0).astype(BF16)
            d["res"] = _dot(stacked, _block_diag(d["npow"], bdm))
        for d in probs:
            res = d.pop("res")
            d["tm"] = d["tm"] + res[c:2 * c, :]
            d["npow"] = res[0:c, :]
    for d in probs:
        rhs = jnp.concatenate([_block_diag(d.pop("a_t"), bdm), _block_diag(d.pop("w1"), bdm)], axis=1)
        d["res2"] = _dot(d.pop("tm").astype(BF16), rhs)
    for d in probs:
        res2 = d.pop("res2")
        lhs_s[d["q"], 0:c, :] = res2[:, 0:2 * hw].astype(BF16)
        uhat_s[d["q"]] = res2[:, 2 * hw:4 * hw]


def _wkv_scan(ci, s_ref, y_ref, v_ref, lhs_s, uhat_s, arbk_s, bkd_s, ecl_s):
    c = WKV_CHUNK
    _, _, _, bdm = _wkv_masks()
    rows = pl.ds(pl.multiple_of(ci * c, c), c)
    probs = []
    for p in range(NPAIR):
        q = ci * NPAIR + p
        s0 = s_ref[p]
        probs.append(dict(p=p, q=q, s0=s0, res3=_dot_nt(lhs_s[q], s0.astype(BF16))))
    for d in probs:
        sl = slice(d["p"] * LANES, (d["p"] + 1) * LANES)
        res3 = d.pop("res3")
        u = res3[0:c, :] + uhat_s[d["q"]]
        v = v_ref[rows, sl]
        d["y"] = res3[c:2 * c, :] + _dot(arbk_s[d["q"]],
                                         jnp.concatenate([_block_diag(u, bdm), _block_diag(v, bdm)], axis=0))
        d["upd"] = _dot_tn(jnp.concatenate([u, v], axis=0).astype(BF16), bkd_s[d["q"]])
    for d in probs:
        sl = slice(d["p"] * LANES, (d["p"] + 1) * LANES)
        y_ref[rows, sl] = d["y"]
        s_ref[d["p"]] = d["s0"] * ecl_s[d["q"]] + jnp.where(bdm, d["upd"], 0.0)


def _rwkv_prompt_kernel(*refs, has_vres, nchunk):
    it = iter(refs)
    prkv_ref, plora_ref = next(it), next(it)
    vfirst_ref = next(it) if has_vres else None
    bones_ref, ltri_ref = next(it), next(it)
    prm = {name: next(it) for name in _RWKV_PARAM_NAMES}
    yc_ref = next(it)
    vout_ref = None if has_vres else next(it)
    s_ref = next(it)
    carry_rkv, carry_lora = next(it), next(it)
    r_s, lw_s, kmod_s, v_s, nkk_s, beta_s, y_s = (next(it) for _ in range(7))
    lhs_s, uhat_s, arbk_s, bkd_s, ecl_s = (next(it) for _ in range(5))

    i = pl.program_id(1)

    @pl.when(i == 0)
    def _():
        carry_rkv[...] = jnp.zeros_like(carry_rkv)
        carry_lora[...] = jnp.zeros_like(carry_lora)
        s_ref[...] = jnp.zeros_like(s_ref)

    prkv = prkv_ref[...]
    plora = plora_ref[...]
    tr = prkv.shape[0]
    bones = bones_ref[...]
    r, lw, kmod, v, nkk, beta, g, bonus, v0 = _rwkv_prep(
        prkv, plora, _shift_rows(prkv, carry_rkv[...]), _shift_rows(plora, carry_lora[...]),
        prm, vfirst_ref[...] if has_vres else None, bones)
    carry_rkv[...] = prkv[tr - 1:tr, :]
    carry_lora[...] = plora[tr - 1:tr, :]
    r_s[...] = r
    lw_s[...] = lw
    kmod_s[...] = kmod
    v_s[...] = v
    nkk_s[...] = nkk
    beta_s[...] = beta
    if not has_vres:
        vout_ref[...] = v0
    ltri = ltri_ref[...]

    mid = (lhs_s, uhat_s, arbk_s, bkd_s, ecl_s)

    def setup(gi, carry):
        chunks = [gi * WKV_SETUP_CHUNKS + j for j in range(WKV_SETUP_CHUNKS)]
        _wkv_setup(chunks, ltri, r_s, lw_s, kmod_s, v_s, nkk_s, beta_s, *mid)
        return carry

    def scan(ci, carry):
        _wkv_scan(ci, s_ref, y_s, v_s, *mid)
        return carry

    lax.fori_loop(0, nchunk // WKV_SETUP_CHUNKS, setup, 0)
    lax.fori_loop(0, nchunk, scan, 0)
    yc_ref[...] = _rwkv_post(y_s[...], bonus, g, prm, bones).astype(yc_ref.dtype)


def _rwkv_consts():
    bones = (jnp.arange(C_C)[:, None] // N_C == jnp.arange(C_C)[None, :] // N_C).astype(BF16)
    ltri = (jnp.arange(WKV_CHUNK)[:, None] >= jnp.arange(WKV_CHUNK)[None, :]).astype(BF16)
    return bones, ltri


def _rwkv_prompt(prkv, plora, v_first, prm, n, s):
    tr = min(ROW_TILE, s)
    nt = s // tr
    has_vres = v_first is not None
    bones, ltri = _rwkv_consts()
    row = lambda w: pl.BlockSpec((tr, w), lambda b, i: (b * nt + i, 0))
    ins = [prkv, plora] + ([v_first] if has_vres else []) + [bones, ltri] + [prm[k] for k in _RWKV_PARAM_NAMES]
    in_specs = ([row(3 * C_C), row(LORA_COLS)] + ([row(C_C)] if has_vres else [])
                + [_resident(a.shape) for a in ins[(3 if has_vres else 2):]])
    npair = NPAIR
    nprob = (tr // WKV_CHUNK) * npair
    c = WKV_CHUNK
    st_spec = pl.BlockSpec((None, npair, LANES, LANES), lambda b, i: (b, 0, 0, 0))
    st_shape = jax.ShapeDtypeStruct((n, npair, LANES, LANES), F32)
    vout_spec = [] if has_vres else [row(C_C)]
    vout_shape = [] if has_vres else [jax.ShapeDtypeStruct((n * s, C_C), F32)]
    res = pl.pallas_call(
        functools.partial(_rwkv_prompt_kernel, has_vres=has_vres, nchunk=tr // WKV_CHUNK),
        grid=(n, nt),
        in_specs=in_specs,
        out_specs=[row(C_C)] + vout_spec + [st_spec],
        out_shape=[jax.ShapeDtypeStruct((n * s, C_C), BF16)] + vout_shape + [st_shape],
        scratch_shapes=[pltpu.VMEM((1, 3 * C_C), F32), pltpu.VMEM((1, LORA_COLS), F32)]
                       + [pltpu.VMEM((tr, C_C), F32) for _ in range(7)]
                       + [pltpu.VMEM((nprob, 2 * c, LANES), BF16), pltpu.VMEM((nprob, c, LANES), F32),
                          pltpu.VMEM((nprob, c, 2 * LANES), BF16), pltpu.VMEM((nprob, 2 * c, LANES), BF16),
                          pltpu.VMEM((nprob, 1, LANES), F32)],
        compiler_params=_cparams("parallel", "arbitrary"),
        name="rwkv_prompt",
    )(*ins)
    yc, s_bd = res[0], res[-1]
    vout = None if has_vres else res[1]
    s6 = s_bd.reshape(n, npair, 2, N_C, 2, N_C)
    s_fin = jnp.stack([s6[:, :, 0, :, 0, :], s6[:, :, 1, :, 1, :]], axis=2).reshape(n, H_C, N_C, N_C)
    return yc, vout, s_fin


def _mix_kernel(*refs, merged):
    x_ref, mod_ref, ya_ref = refs[0:3]
    nb_in = 1 if merged else 6
    b_refs = refs[3:3 + nb_in]
    (yc_ref, wg_ref, bg_ref, wa_ref, wb_ref, wc_ref, wo_ref, lng_ref, lnb_ref, out_ref) = refs[3 + nb_in:]
    x = x_ref[...]
    d = x.shape[1]
    u = (x * (1.0 + mod_ref[1]) + mod_ref[0]).astype(BF16)
    if merged:
        yb = b_refs[0][...]
    else:
        rows = slice(None)
        o0, o1, o2, l0, l1, l2 = (_lane_tiles(ref, rows, 0) for ref in b_refs)
        m = jnp.maximum(jnp.maximum(l0, l1), l2)
        e0, e1, e2 = jnp.exp(l0 - m), jnp.exp(l1 - m), jnp.exp(l2 - m)
        den = e0 + e1 + e2
        yb = (e0 / den) * o0 + (e1 / den) * o1 + (e2 / den) * o2
    branches = (_dot(ya_ref[...].astype(BF16), wa_ref[...]),
                _dot(yb.astype(BF16), wb_ref[...]),
                _dot(yc_ref[...].astype(BF16), wc_ref[...]))
    mix = None
    for j, br in enumerate(branches):
        gate = _sigmoid(_dot(u, wg_ref[:, j * d:(j + 1) * d]) + bg_ref[:, j * d:(j + 1) * d])
        mix = gate * br if mix is None else mix + gate * br
    h = _dot(mix.astype(BF16), wo_ref[...])
    y = ALPHA * x + (1.0 + mod_ref[2]) * h
    out_ref[...] = _layer_norm(y, lng_ref[...], lnb_ref[...])


def _mix(x, mod, ya, yb_parts, yc, wg, bg, wa, wb, wc, wo, ln_g, ln_b, tiles_per_seq):
    r, d = x.shape
    tm = min(ROW_TILE, r)
    row = lambda w: pl.BlockSpec((tm, w), lambda i: (i, 0))
    tiles = pl.BlockSpec((B_TILES, tm, LANES), lambda i: (0, i, 0))
    weights = [wg, bg, wa, wb, wc, wo, ln_g, ln_b]
    ins = [x, mod, ya] + list(yb_parts) + [yc] + weights
    in_specs = ([row(d), _mod_spec(mod, tm, tiles_per_seq), row(A_V)]
                + ([row(B_W)] if len(yb_parts) == 1 else [tiles] * len(yb_parts))
                + [row(C_C)] + [_resident(a.shape) for a in weights])
    return pl.pallas_call(
        functools.partial(_mix_kernel, merged=len(yb_parts) == 1),
        grid=(r // tm,),
        in_specs=in_specs,
        out_specs=row(d),
        out_shape=jax.ShapeDtypeStruct((r, d), F32),
        compiler_params=_cparams("parallel"),
        name="mix",
    )(*ins)


RET_SAMPLE_SEQS = 4


def _ret_sample_kernel(q_ref, k_ref, v_ref, dm_ref, qd_ref, kd_ref, gc_ref, st_ref, y_ref, so_ref):
    probs = []
    for b in range(q_ref.shape[0]):
        for h in range(H_A):
            q, k, v = q_ref[b, h], k_ref[b, h], v_ref[b, h].astype(BF16)
            probs.append(dict(b=b, h=h, v=v, att=_dot_nt(q.astype(BF16), k.astype(BF16)) * dm_ref[h],
                              cross=_dot((q * qd_ref[h]).astype(BF16), st_ref[b, h].astype(BF16)),
                              kv=_dot_tn((k * kd_ref[h]).astype(BF16), v)))
    for d in probs:
        b, h = d["b"], d["h"]
        y_ref[b, h] = _dot(d["att"].astype(BF16), d["v"]) + d["cross"]
        so_ref[b, h] = st_ref[b, h] * gc_ref[h] + d["kv"]


def _ret_sample(q, k, v, states, layer, t):
    nb = q.shape[0]
    sb = RET_SAMPLE_SEQS
    dmask, kdec, qdec, gc = _ret_tables(t)
    padt = TOK_PAD - t
    dm = jnp.pad(dmask, ((0, 0), (0, padt), (0, padt)))
    qd = jnp.broadcast_to(jnp.pad(qdec, ((0, 0), (0, padt)))[:, :, None], (H_A, TOK_PAD, DK_A))
    kd = jnp.broadcast_to(jnp.pad(kdec, ((0, 0), (0, padt)))[:, :, None], (H_A, TOK_PAD, DK_A))
    gcb = jnp.broadcast_to(gc[:, None, None], (H_A, DK_A, DV_A))
    tok = lambda w: pl.BlockSpec((sb, H_A, TOK_PAD, w), lambda i: (i, 0, 0, 0))
    return pl.pallas_call(
        _ret_sample_kernel,
        grid=(nb // sb,),
        in_specs=[tok(DK_A), tok(DK_A), tok(DV_A), _resident(dm.shape), _resident(qd.shape),
                  _resident(kd.shape), _resident(gcb.shape),
                  pl.BlockSpec((None, sb, H_A, DK_A, DV_A), lambda i: (layer, i, 0, 0, 0))],
        out_specs=[tok(DV_A), pl.BlockSpec((sb, H_A, DK_A, DV_A), lambda i: (i, 0, 0, 0))],
        out_shape=[jax.ShapeDtypeStruct((nb, H_A, TOK_PAD, DV_A), F32),
                   jax.ShapeDtypeStruct(states.shape[1:], F32)],
        compiler_params=_cparams("parallel"),
        name="ret_sample",
    )(q, k, v, dm, qd, kd, gcb, states)


def _ret_norm_kernel(y_ref, g_ref, gng_ref, gnb_ref, o_ref):
    y = y_ref[...]
    outs = []
    for h in range(H_A):
        yh = y[:, h * DV_A:(h + 1) * DV_A]
        mu = jnp.mean(yh, axis=-1, keepdims=True)
        dd = yh - mu
        var = jnp.mean(dd * dd, axis=-1, keepdims=True)
        outs.append(dd * lax.rsqrt(var + GN_EPS_A))
    yn = jnp.concatenate(outs, axis=1) * gng_ref[...] + gnb_ref[...]
    o_ref[...] = (yn * _silu(g_ref[...].astype(F32))).astype(o_ref.dtype)


def _ret_norm(y, g, gn_g, gn_b):
    r = y.shape[0]
    full = lambda a: pl.BlockSpec(a.shape, lambda: (0,) * a.ndim)
    return pl.pallas_call(
        _ret_norm_kernel,
        in_specs=[full(y), full(g), full(gn_g), full(gn_b)],
        out_specs=pl.BlockSpec((r, A_V), lambda: (0, 0)),
        out_shape=jax.ShapeDtypeStruct((r, A_V), BF16),
        name="ret_norm",
    )(y, g, gn_g, gn_b)


TOK_PAD = 8


def _win_sample_kernel(*refs, t):
    ng = len(DIL_GROUPS)
    q_refs, kn_refs, vn_refs = refs[0:ng], refs[ng:2 * ng], refs[2 * ng:3 * ng]
    kv_refs, w_refs = refs[3 * ng:4 * ng], refs[4 * ng:5 * ng]
    base = 6 * ng
    yb_ref, o_refs = refs[base], refs[base + 1:base + 1 + ng]
    nrow = TOK_PAD
    ncol = lax.broadcasted_iota(jnp.int32, (nrow, nrow), 1)
    ntok = lax.broadcasted_iota(jnp.int32, (nrow, nrow), 0)
    probs = []
    for g, (_, dil) in enumerate(DIL_GROUPS):
        w = w_refs[g].shape[-1]
        lane = lax.broadcasted_iota(jnp.int32, (nrow, w), 1)
        tok = lax.broadcasted_iota(jnp.int32, (nrow, w), 0)
        valid = ((lane >= tok) if dil == 1 else (lane % dil == tok)) | (tok >= t)
        ok_new = (ncol <= ntok) if dil == 1 else (ncol == ntok)
        for h in range(H_B):
            q = q_refs[g][h].astype(BF16)
            probs.append(dict(g=g, h=h, valid=valid, ok_new=ok_new,
                              sc=_dot(q, w_refs[g][0, h].astype(BF16)),
                              sn=_dot_nt(q, kn_refs[g][h].astype(BF16))))
    for d in probs:
        sc = jnp.where(d.pop("valid"), d.pop("sc") * DH_B ** -0.5, -jnp.inf)
        sn = jnp.where(d.pop("ok_new"), d.pop("sn") * DH_B ** -0.5, -jnp.inf)
        m = jnp.maximum(jnp.max(sc, axis=-1, keepdims=True), jnp.max(sn, axis=-1, keepdims=True))
        p = jnp.exp(sc - m)
        pn = jnp.exp(sn - m)
        l = jnp.sum(p, axis=-1, keepdims=True) + jnp.sum(pn, axis=-1, keepdims=True)
        d["lse"] = m + jnp.log(l)
        d["l"] = l
        d["p"] = p.astype(BF16)
        d["pn"] = pn.astype(BF16)
    for d in probs:
        g, h = d["g"], d["h"]
        d["o"] = (_dot_nt(d.pop("p"), w_refs[g][1, h].astype(BF16))
                  + _dot(d.pop("pn"), vn_refs[g][h].astype(BF16))) / d.pop("l")
    for h in range(H_B):
        mine = [d for d in probs if d["h"] == h]
        m = functools.reduce(jnp.maximum, [d["lse"] for d in mine])
        es = [jnp.exp(d["lse"] - m) for d in mine]
        den = functools.reduce(lambda a, b: a + b, es)
        yb_ref[h] = functools.reduce(lambda a, b: a + b, [(e / den) * d["o"] for e, d in zip(es, mine)])
    sel = (lax.broadcasted_iota(jnp.int32, (nrow, LANES), 0) + (LANES - t)
           == lax.broadcasted_iota(jnp.int32, (nrow, LANES), 1)).astype(BF16)
    last = lax.broadcasted_iota(jnp.int32, (DH_B, LANES), 1) >= LANES - t
    for g in range(ng):
        w = w_refs[g].shape[-1]
        new = kv_refs[g][...]
        p1 = new.astype(BF16)
        r1 = new - p1.astype(F32)
        p2 = r1.astype(BF16)
        p3 = (r1 - p2.astype(F32)).astype(BF16)
        cols = _dot_tn(p1, sel) + _dot_tn(p2, sel) + _dot_tn(p3, sel)
        for kv in range(2):
            for h in range(H_B):
                rolled = pltpu.roll(w_refs[g][kv, h], w - t, 1)
                c0 = (kv * H_B + h) * DH_B
                if w > LANES:
                    o_refs[g][kv, h, :, 0:w - LANES] = rolled[:, 0:w - LANES]
                o_refs[g][kv, h, :, w - LANES:w] = jnp.where(last, cols[c0:c0 + DH_B, :],
                                                             rolled[:, w - LANES:w])


def _win_sample(qkv, wins_t, prev, layer, nb, t):
    ng = len(DIL_GROUPS)

    def heads(a):
        a = a.reshape(nb, t, H_B, DH_B).transpose(0, 2, 1, 3)
        return jnp.pad(a, ((0, 0), (0, 0), (0, TOK_PAD - t), (0, 0)))

    qs = [heads(x[:, 0:B_W]) for x in qkv]
    kns = [heads(x[:, B_W:2 * B_W]) for x in qkv]
    vns = [heads(x[:, 2 * B_W:3 * B_W]) for x in qkv]
    kvs = [jnp.pad(x[:, B_W:].reshape(nb, t, 2 * B_W), ((0, 0), (0, TOK_PAD - t), (0, 0))) for x in qkv]
    hspec = pl.BlockSpec((None, H_B, TOK_PAD, DH_B), lambda b: (b, 0, 0, 0))
    kvspec = pl.BlockSpec((None, TOK_PAD, 2 * B_W), lambda b: (b, 0, 0))
    wspec = lambda a: pl.BlockSpec((None, None) + a.shape[2:], lambda b: (layer, b, 0, 0, 0, 0))
    ins = qs + kns + vns + kvs + list(wins_t) + list(prev)
    in_specs = ([hspec] * (3 * ng) + [kvspec] * ng + [wspec(a) for a in wins_t]
                + [pl.BlockSpec(memory_space=pl.ANY)] * ng)
    outs = pl.pallas_call(
        functools.partial(_win_sample_kernel, t=t),
        grid=(nb,),
        in_specs=in_specs,
        out_specs=[hspec] + [wspec(a) for a in wins_t],
        out_shape=[jax.ShapeDtypeStruct((nb, H_B, TOK_PAD, DH_B), F32)]
                  + [jax.ShapeDtypeStruct(a.shape, a.dtype) for a in wins_t],
        input_output_aliases={5 * ng + g: 1 + g for g in range(ng)},
        compiler_params=_cparams("parallel"),
        name="win_sample",
    )(*ins)
    yb = outs[0][:, :, :t].transpose(0, 2, 1, 3).reshape(nb * t, B_W)
    return yb, outs[1:]


def _rwkv_prep_kernel(*refs, has_vres):
    it = iter(refs)
    prkv_ref, plora_ref, prev_rkv_ref, prev_lora_ref = (next(it) for _ in range(4))
    vfirst_ref = next(it) if has_vres else None
    bones_ref = next(it)
    prm = {name: next(it) for name in _RWKV_PARAM_NAMES}
    outs = [next(it) for _ in range(9)]
    res = _rwkv_prep(prkv_ref[...], plora_ref[...], prev_rkv_ref[...], prev_lora_ref[...], prm,
                     vfirst_ref[...] if has_vres else None, bones_ref[...])
    r, lw, kmod, v, nkk, beta, g, bonus, v0 = res
    for ref, val in zip(outs, (r, jnp.exp(lw), kmod, v, nkk, beta, g, bonus, v0)):
        ref[...] = val


def _rwkv_prep_call(prkv, plora, prev_rkv, prev_lora, v_first, prm):
    has_vres = v_first is not None
    bones, _ = _rwkv_consts()
    ins = [prkv, plora, prev_rkv, prev_lora] + ([v_first] if has_vres else []) + [bones] + [prm[k] for k in _RWKV_PARAM_NAMES]
    full = lambda a: pl.BlockSpec(a.shape, lambda: (0,) * a.ndim)
    r = prkv.shape[0]
    return pl.pallas_call(
        functools.partial(_rwkv_prep_kernel, has_vres=has_vres),
        in_specs=[full(a) for a in ins],
        out_specs=[pl.BlockSpec((r, C_C), lambda: (0, 0))] * 9,
        out_shape=[jax.ShapeDtypeStruct((r, C_C), F32)] * 9,
        compiler_params=pltpu.CompilerParams(vmem_limit_bytes=VMEM_LIMIT),
        name="rwkv_prep",
    )(*ins)


def _wkv_sample_kernel(r_ref, w_ref, k_ref, v_ref, nkk_ref, beta_ref, st_ref, y_ref, so_ref, *, t):
    for ti in range(t):
        src = st_ref if ti == 0 else so_ref
        nkk_t, w_t, beta_t, k_t, r_t = nkk_ref[ti], w_ref[ti], beta_ref[ti], k_ref[ti], r_ref[ti]

        def step(vi, carry, src=src, ti=ti, nkk_t=nkk_t, w_t=w_t, beta_t=beta_t, k_t=k_t, r_t=r_t):
            s = src[vi]
            sa = jnp.sum(s * nkk_t, axis=0, keepdims=True)
            s_new = s * w_t + sa * beta_t + v_ref[ti, pl.ds(vi, 1), :] * k_t
            so_ref[vi] = s_new
            y_ref[ti, pl.ds(vi, 1), :] = jnp.sum(s_new * r_t, axis=0, keepdims=True)
            return carry

        lax.fori_loop(0, N_C, step, 0, unroll=4)


def _wkv_sample(r, w, k, v, nkk, beta, states, layer, t):
    nb = r.shape[-1]
    per_head = pl.BlockSpec((t, None, N_C, nb), lambda h: (0, h, 0, 0))
    return pl.pallas_call(
        functools.partial(_wkv_sample_kernel, t=t),
        grid=(H_C,),
        in_specs=[per_head] * 6 + [pl.BlockSpec((None, None, N_C, N_C, nb), lambda h: (layer, h, 0, 0, 0))],
        out_specs=[per_head, pl.BlockSpec((None, N_C, N_C, nb), lambda h: (h, 0, 0, 0))],
        out_shape=[jax.ShapeDtypeStruct((t, H_C, N_C, nb), F32),
                   jax.ShapeDtypeStruct(states.shape[1:], F32)],
        compiler_params=_cparams("parallel"),
        name="wkv_sample",
    )(r, w, k, v, nkk, beta, states)


def _rwkv_post_kernel(y_ref, bonus_ref, g_ref, bones_ref, lnxg_ref, lnxb_ref, o_ref):
    prm = {"lnx_g": lnxg_ref, "lnx_b": lnxb_ref}
    o_ref[...] = _rwkv_post(y_ref[...], bonus_ref[...], g_ref[...], prm, bones_ref[...]).astype(o_ref.dtype)


def _rwkv_post_call(y, bonus, g, prm):
    bones, _ = _rwkv_consts()
    ins = [y, bonus, g, bones, prm["lnx_g"], prm["lnx_b"]]
    full = lambda a: pl.BlockSpec(a.shape, lambda: (0,) * a.ndim)
    r = y.shape[0]
    return pl.pallas_call(
        _rwkv_post_kernel,
        in_specs=[full(a) for a in ins],
        out_specs=pl.BlockSpec((r, C_C), lambda: (0, 0)),
        out_shape=jax.ShapeDtypeStruct((r, C_C), BF16),
        name="rwkv_post",
    )(*ins)


def _rotary_tables(pos):
    half = DK_A // 2
    inv = 1.0 / (ROPE_BASE_A ** jnp.linspace(0.0, 1.0, half, dtype=F32))
    ang = pos.astype(F32)[:, None] * inv[None, :]
    cos, sin = jnp.cos(ang), jnp.sin(ang)
    return jnp.tile(cos, (1, 4)), jnp.tile(jnp.concatenate([-sin, sin], axis=1), (1, 2))


def _rwkv_params(l, w):
    row = lambda a: a.reshape(1, -1)
    pad_rows = lambda a, r0: jnp.zeros((LORA_COLS, a.shape[1]), BF16).at[r0:r0 + a.shape[0]].set(a.astype(BF16))
    prm = {
        "mu_rkv": row(w["rw_mu"][l, :3 * C_C]), "mu_lora": row(w["rw_mu"][l, 3 * C_C:]),
        "w0": row(w["rw_w0"][l]), "w2": pad_rows(w["rw_w2"][l], 0),
        "a0": row(w["rw_a0"][l]), "a2": pad_rows(w["rw_a2"][l], LORA_W),
        "g2": pad_rows(w["rw_g2"][l], LORA_W + LORA_A),
        "k_k": row(w["rw_kk"][l]), "k_a": row(w["rw_ka"][l]), "r_k": row(w["rw_rk"][l]),
        "lnx_g": row(w["rw_lnx_g"][l]), "lnx_b": row(w["rw_lnx_b"][l]),
    }
    lv = max(l - 1, 0)
    prm["v0"] = row(w["rw_v0"][lv])
    prm["vr1"] = w["rw_vr1"][lv].astype(BF16)
    prm["vr2"] = w["rw_vr2"][lv].astype(BF16)
    return prm


def _layer_weights(l, w):
    d = w["w_in"].shape[1]
    return dict(
        ffn1_up=w["ffn1_up"][l].astype(BF16), ffn1_dn=w["ffn1_down"][l].astype(BF16),
        ffn2_up=w["ffn2_up"][l].astype(BF16), ffn2_dn=w["ffn2_down"][l].astype(BF16),
        w_mix=w["w_in"][l, :, :G_OFF].astype(BF16), w_gate=w["w_in"][l, :, G_OFF:].astype(BF16),
        b_gate=w["b_gate"][l].reshape(1, -1),
        wa=w["w_proj_a"][l].astype(BF16), wb=w["w_proj_b"][l].astype(BF16),
        wc=w["w_proj_c"][l].astype(BF16), wo=w["w_out"][l].astype(BF16),
        ln_g=[w["ln_g"][l, j].reshape(1, d) for j in range(3)],
        ln_b=[w["ln_b"][l, j].reshape(1, d) for j in range(3)],
        gn_g=w["ret_gn_g"][l].reshape(1, -1), gn_b=w["ret_gn_b"][l].reshape(1, -1),
        rwkv=_rwkv_params(l, w),
    )


def _prompt_layer(x, m, lw, tabs, v_first, n, s):
    tps = s // min(ROW_TILE, s)
    x = _ffn(x, m[0:3], lw["ffn1_up"], lw["ffn1_dn"], lw["ln_g"][0], lw["ln_b"][0], tps)
    qa, ka, va, ga, b0, b1, b2, prkv, plora = _proj(x, m[3:6], lw["w_mix"], tabs[0], tabs[1], tps)
    ya, r_fin = _ret_prompt(qa, ka, va, ga, lw["gn_g"], lw["gn_b"], n, s)
    ob, lb, wins = [], [], []
    for qkv, (win, dil) in zip((b0, b1, b2), DIL_GROUPS):
        o, lse = _dil_prompt(qkv, n, s, dil)
        ob.append(o)
        lb.append(lse)
        wp = min(win, s)
        kv = qkv.reshape(3, B_TILES, n, s, LANES)[1:3, :, :, s - wp:]
        kv = kv.reshape(2, B_TILES, n, wp, H_B // B_TILES, DH_B)
        wins.append(kv.transpose(2, 3, 0, 1, 4, 5).reshape(n, wp, 2, H_B, DH_B))
    yc, v_out, s_fin = _rwkv_prompt(prkv, plora, v_first, lw["rwkv"], n, s)
    if v_first is None:
        v_first = v_out
    sh_new = jnp.concatenate([prkv.reshape(n, s, -1)[:, -1], plora.reshape(n, s, -1)[:, -1]], axis=-1)
    x = _mix(x, m[3:6], ya, ob + lb, yc, lw["w_gate"], lw["b_gate"], lw["wa"], lw["wb"], lw["wc"], lw["wo"],
             lw["ln_g"][1], lw["ln_b"][1], tps)
    x = _ffn(x, m[6:9], lw["ffn2_up"], lw["ffn2_dn"], lw["ln_g"][2], lw["ln_b"][2], tps)
    return x, v_first, (r_fin, s_fin, sh_new, wins)


def _to_lanes(a, nb, t, heads):
    return a.reshape(nb, t, heads, -1).transpose(1, 2, 3, 0)


def _from_lanes(a):
    t, heads, w, nb = a.shape
    return a.transpose(3, 0, 1, 2).reshape(nb * t, heads * w)


def _token_heads(a, nb, t, heads):
    a = a.astype(F32).reshape(nb, t, heads, -1).transpose(0, 2, 1, 3)
    return jnp.pad(a, ((0, 0), (0, 0), (0, TOK_PAD - t), (0, 0)))


def _sample_layer(x, m, lw, tabs, v_first, l, past, new_wins, nb, t):
    state_ret, state_wkv_t, state_shift, wins_t = past
    x = _ffn(x, m[0:3], lw["ffn1_up"], lw["ffn1_dn"], lw["ln_g"][0], lw["ln_b"][0], 1)
    qa, ka, va, ga, b0, b1, b2, prkv, plora = _proj(x, m[3:6], lw["w_mix"], tabs[0], tabs[1], 1)
    b0, b1, b2 = (b.transpose(1, 0, 2).reshape(nb * t, 3 * B_W) for b in (b0, b1, b2))
    y_h, r_new = _ret_sample(_token_heads(qa, nb, t, H_A), _token_heads(ka, nb, t, H_A),
                             _token_heads(va, nb, t, H_A), state_ret, l, t)
    ya = _ret_norm(y_h[:, :, :t].transpose(0, 2, 1, 3).reshape(nb * t, A_V), ga, lw["gn_g"], lw["gn_b"])
    yb, new_wins = _win_sample((b0, b1, b2), wins_t, new_wins, l, nb, t)
    prev_rkv = jnp.concatenate([state_shift[l][:, None, :3 * C_C], prkv.reshape(nb, t, -1)[:, :-1]], axis=1)
    prev_lora = jnp.concatenate([state_shift[l][:, None, 3 * C_C:], plora.reshape(nb, t, -1)[:, :-1]], axis=1)
    r, dec, kmod, v, nkk, beta, g, bonus, v0 = _rwkv_prep_call(
        prkv, plora, prev_rkv.reshape(nb * t, -1), prev_lora.reshape(nb * t, -1), v_first, lw["rwkv"])
    if v_first is None:
        v_first = v0
    y_l, s_new_l = _wkv_sample(*[_to_lanes(a, nb, t, H_C) for a in (r, dec, kmod, v, nkk, beta)],
                               state_wkv_t, l, t)
    s_new = s_new_l.transpose(3, 0, 1, 2)
    yc = _rwkv_post_call(_from_lanes(y_l), bonus, g, lw["rwkv"])
    sh_new = jnp.concatenate([prkv.reshape(nb, t, -1)[:, -1], plora.reshape(nb, t, -1)[:, -1]], axis=-1)
    x = _mix(x, m[3:6], ya, [yb], yc, lw["w_gate"], lw["b_gate"], lw["wa"], lw["wb"], lw["wc"], lw["wo"],
             lw["ln_g"][1], lw["ln_b"][1], 1)
    x = _ffn(x, m[6:9], lw["ffn2_up"], lw["ffn2_dn"], lw["ln_g"][2], lw["ln_b"][2], 1)
    return x, v_first, (r_new, s_new, sh_new), new_wins


def kernel(x_prompt, x_sample, state_ret, state_wkv, state_shift, cache_win1, cache_win2, cache_win3, c_prompt, c_sample, w_ada, b_ada, ln_g, ln_b, ffn1_up, ffn1_down, ffn2_up, ffn2_down, w_in, b_gate, ret_gn_g, ret_gn_b, rw_mu, rw_w0, rw_w2, rw_a0, rw_a2, rw_g2, rw_kk, rw_ka, rw_rk, rw_lnx_g, rw_lnx_b, rw_v0, rw_vr1, rw_vr2, w_proj_a, w_proj_b, w_proj_c, w_out):
    w = dict(w_ada=w_ada, b_ada=b_ada, ln_g=ln_g, ln_b=ln_b, ffn1_up=ffn1_up, ffn1_down=ffn1_down,
             ffn2_up=ffn2_up, ffn2_down=ffn2_down, w_in=w_in, b_gate=b_gate, ret_gn_g=ret_gn_g,
             ret_gn_b=ret_gn_b, rw_mu=rw_mu, rw_w0=rw_w0, rw_w2=rw_w2, rw_a0=rw_a0, rw_a2=rw_a2,
             rw_g2=rw_g2, rw_kk=rw_kk, rw_ka=rw_ka, rw_rk=rw_rk, rw_lnx_g=rw_lnx_g, rw_lnx_b=rw_lnx_b,
             rw_v0=rw_v0, rw_vr1=rw_vr1, rw_vr2=rw_vr2, w_proj_a=w_proj_a, w_proj_b=w_proj_b,
             w_proj_c=w_proj_c, w_out=w_out)
    n, s, d = x_prompt.shape
    nb, t, _ = x_sample.shape
    depth = w_ada.shape[0]
    past_len = cache_win3.shape[2]

    m_all = _ada(jnp.concatenate([c_prompt, c_sample], axis=0), w_ada, b_ada).reshape(depth, n + nb, 9, d)
    tabs_p = _rotary_tables(jnp.arange(s))
    tabs_s = tuple(jnp.tile(tb, (nb, 1)) for tb in _rotary_tables(past_len + jnp.arange(t)))
    caches = (cache_win1, cache_win2, cache_win3)
    for cache, (win, _) in zip(caches, DIL_GROUPS):
        assert cache.shape[2] == win, "each window buffer must hold its whole window"
    wins_t = tuple(c.transpose(0, 1, 3, 4, 5, 2) for c in caches)
    past = (state_ret, state_wkv.transpose(0, 2, 3, 4, 1), state_shift, wins_t)

    xp = x_prompt.reshape(n * s, d)
    xs = x_sample.reshape(nb * t, d)
    vf_p = vf_s = None
    new_wins = [jnp.zeros(a.shape, a.dtype) for a in wins_t]
    outs_p, outs_s = [], []
    for l in range(depth):
        lw = _layer_weights(l, w)
        m_p = m_all[l, :n].transpose(1, 0, 2)[:, :, None, :]
        m_s = jnp.repeat(m_all[l, n:], t, axis=0).transpose(1, 0, 2)
        xp, vf_p, st_p = _prompt_layer(xp, m_p, lw, tabs_p, vf_p, n, s)
        xs, vf_s, st_s, new_wins = _sample_layer(xs, m_s, lw, tabs_s, vf_s, l, past, new_wins, nb, t)
        outs_p.append(st_p)
        outs_s.append(st_s)

    stack = lambda outs, j: jnp.stack([o[j] for o in outs])
    p_wins = [jnp.stack([o[3][gi] for o in outs_p]) for gi in range(len(DIL_GROUPS))]
    s_wins = [a.transpose(0, 1, 5, 2, 3, 4) for a in new_wins]
    return (xp.reshape(n, s, d), xs.reshape(nb, t, d), stack(outs_p, 0), stack(outs_p, 1), stack(outs_p, 2),
            *p_wins, stack(outs_s, 0), stack(outs_s, 1), stack(outs_s, 2), *s_wins)
```
